```python
import jax, jax.numpy as jnp
from jax import lax
import numpy as np

D_MODEL = 1024
BATCH = 2
SEQ = 8192
DEPTH = 4

N_MEM = 256
EPS = 1e-6
N_BRANCH = 4
BRANCH_W = D_MODEL // 2
GM_W = BRANCH_W
GM_GROUPS = 4
GM_CHUNK = 128
LRU_W = BRANCH_W
LRU_BLOCKS = 8
LRU_CONV = 4
LRU_C = 8.0
HEAD_DIM = 64
SWA_HEADS = BRANCH_W // HEAD_DIM
SWA_KV = 2
WINDOW = 128
ROPE_THETA = 500000.0
ROT_DIM = HEAD_DIM // 4
XA_HEADS = 4
XA_DIM = BRANCH_W // XA_HEADS
D_FF = 4 * D_MODEL

IN_SIZES = [2 * GM_W, 2 * LRU_W, SWA_HEADS * HEAD_DIM, SWA_KV * HEAD_DIM, SWA_KV * HEAD_DIM,
            XA_HEADS * XA_DIM, N_BRANCH * D_MODEL]
D_IN = sum(IN_SIZES)
IN_SPLITS = [int(v) for v in np.cumsum(IN_SIZES)[:-1]]

kernel_name = "griffin_hybrid_gmlp_rglru_swa_xattn"


def rmsnorm(x, g):
    xf = x.astype(jnp.float32)
    y = xf * lax.rsqrt(jnp.mean(xf * xf, axis=-1, keepdims=True) + EPS)
    return (y * g.astype(jnp.float32)).astype(x.dtype)


def rope_tables(seq):
    pos = jnp.arange(seq, dtype=jnp.float32)
    inv = ROPE_THETA ** (-jnp.arange(0, ROT_DIM, 2, dtype=jnp.float32) / ROT_DIM)
    ang = pos[:, None] * inv[None, :]
    return jnp.cos(ang), jnp.sin(ang)


def partial_rope(x, cos, sin):
    xf = x.astype(jnp.float32)
    half = ROT_DIM // 2
    x1, x2, xp = xf[..., :half], xf[..., half:ROT_DIM], xf[..., ROT_DIM:]
    c = cos[None, :, None, :]
    s = sin[None, :, None, :]
    out = jnp.concatenate([x1 * c - x2 * s, x2 * c + x1 * s, xp], axis=-1)
    return out.astype(x.dtype)


def gmlp_branch(z, v_gain, ws, bs):
    B, S, _ = z.shape
    z = jax.nn.gelu(z)
    u, v = jnp.split(z, 2, axis=-1)
    v = rmsnorm(v, v_gain)
    nc = S // GM_CHUNK
    v = v.reshape(B, nc, GM_CHUNK, GM_GROUPS, GM_W // GM_GROUPS)
    causal = jnp.tril(jnp.ones((GM_CHUNK, GM_CHUNK), dtype=bool))
    w = jnp.where(causal[None], ws, jnp.zeros_like(ws))
    s = jnp.einsum('gts,bnsgc->bntgc', w, v) + bs.T[:, :, None]
    return u * s.reshape(B, S, GM_W)


def rglru_branch(z, conv_w, conv_b, wr, br, wi, bi, lam):
    B, S, _ = z.shape
    xb, gb = jnp.split(z, 2, axis=-1)
    xp = jnp.pad(xb, ((0, 0), (LRU_CONV - 1, 0), (0, 0)))
    xc = conv_b + xp[:, 0:S] * conv_w[0]
    for k in range(1, LRU_CONV):
        xc = xc + xp[:, k:k + S] * conv_w[k]
    xh = xc.reshape(B, S, LRU_BLOCKS, LRU_W // LRU_BLOCKS)
    r_gate = jax.nn.sigmoid(jnp.einsum('bshi,hij->bshj', xh, wr).reshape(B, S, LRU_W) + br)
    i_gate = jax.nn.sigmoid(jnp.einsum('bshi,hij->bshj', xh, wi).reshape(B, S, LRU_W) + bi)
    log_a = -LRU_C * r_gate.astype(jnp.float32) * jax.nn.softplus(-lam.astype(jnp.float32))
    a = jnp.exp(log_a)
    mult = jnp.sqrt(-jnp.expm1(2.0 * log_a))
    b_in = xc.astype(jnp.float32) * i_gate.astype(jnp.float32) * mult

    def combine(lhs, rhs):
        a1, b1 = lhs
        a2, b2 = rhs
        return a1 * a2, a2 * b1 + b2

    _, h = lax.associative_scan(combine, (a, b_in), axis=1)
    return jax.nn.gelu(gb) * h.astype(gb.dtype)


def swa_branch(q, k, v, q_gain, k_gain, sinks, cos, sin):
    B, S, _ = q.shape
    G = SWA_HEADS // SWA_KV
    nb = S // WINDOW
    q = partial_rope(rmsnorm(q.reshape(B, S, SWA_HEADS, HEAD_DIM), q_gain), cos, sin)
    k = partial_rope(rmsnorm(k.reshape(B, S, SWA_KV, HEAD_DIM), k_gain), cos, sin)
    v = v.reshape(B, S, SWA_KV, HEAD_DIM)
    qb = q.reshape(B, nb, WINDOW, SWA_KV, G, HEAD_DIM)
    kb = k.reshape(B, nb, WINDOW, SWA_KV, HEAD_DIM)
    vb = v.reshape(B, nb, WINDOW, SWA_KV, HEAD_DIM)
    pad = ((0, 0), (1, 0), (0, 0), (0, 0), (0, 0))
    kk = jnp.concatenate([jnp.pad(kb, pad)[:, :-1], kb], axis=2)
    vv = jnp.concatenate([jnp.pad(vb, pad)[:, :-1], vb], axis=2)
    scores = jnp.einsum('bnikgd,bnjkd->bnkgij', qb, kk).astype(jnp.float32) * (HEAD_DIM ** -0.5)
    qi = jnp.arange(WINDOW)[:, None]
    kj = jnp.arange(2 * WINDOW)[None, :]
    diff = qi + WINDOW - kj
    band = (diff >= 0) & (diff < WINDOW)
    first_ok = (jnp.arange(nb)[:, None, None] > 0) | (kj[None] >= WINDOW)
    valid = band[None] & first_ok
    scores = jnp.where(valid[None, :, None, None], scores, -jnp.inf)
    sink = sinks.astype(jnp.float32).reshape(SWA_KV, G)[None, None, :, :, None, None]
    m = jnp.maximum(jnp.max(scores, axis=-1, keepdims=True), sink)
    p = jnp.exp(scores - m)
    p = p / (jnp.sum(p, axis=-1, keepdims=True) + jnp.exp(sink - m))
    o = jnp.einsum('bnkgij,bnjkd->bnikgd', p.astype(vv.dtype), vv)
    return o.reshape(B, S, SWA_HEADS * HEAD_DIM)


def cross_branch(q, mem_n, w_kv, q_gain, k_gain):
    B, S, _ = q.shape
    M = mem_n.shape[1]
    q = rmsnorm(q.reshape(B, S, XA_HEADS, XA_DIM), q_gain)
    k, v = jnp.split(mem_n @ w_kv, 2, axis=-1)
    k = rmsnorm(k.reshape(B, M, XA_HEADS, XA_DIM), k_gain)
    v = v.reshape(B, M, XA_HEADS, XA_DIM)
    s = jnp.einsum('bshd,bmhd->bhsm', q, k).astype(jnp.float32) * (XA_DIM ** -0.5)
    p = jax.nn.softmax(s, axis=-1)
    o = jnp.einsum('bhsm,bmhd->bshd', p.astype(v.dtype), v)
    return o.reshape(B, S, XA_HEADS * XA_DIM)


def setup_inputs(seed: int = 0) -> dict:
    key = jax.random.key(seed)
    ks = jax.random.split(key, 32)
    f32 = jnp.float32
    nrm = lambda k, shape, scale: jax.random.normal(k, shape, f32) * scale
    gain = lambda k, shape: 1.0 + 0.01 * jax.random.normal(k, shape, f32)
    L = DEPTH
    a_c = jax.random.uniform(ks[12], (L, LRU_W), f32, 0.9, 0.999)
    a0 = a_c ** (1.0 / LRU_C)
    lam = jnp.log(a0) - jnp.log1p(-a0)
    return {
        "x": nrm(ks[0], (BATCH, SEQ, D_MODEL), 1.0),
        "mem": nrm(ks[1], (BATCH, N_MEM, D_MODEL), 1.0),
        "norm_mix": gain(ks[2], (L, D_MODEL)),
        "norm_mem": gain(ks[3], (L, D_MODEL)),
        "norm_mlp": gain(ks[4], (L, D_MODEL)),
        "w_in": nrm(ks[5], (L, D_MODEL, D_IN), D_MODEL ** -0.5),
        "b_gate": nrm(ks[6], (L, N_BRANCH, D_MODEL), 0.01),
        "gm_v_gain": gain(ks[7], (L, GM_W)),
        "gm_ws": nrm(ks[8], (L, GM_GROUPS, GM_CHUNK, GM_CHUNK), GM_CHUNK ** -0.5),
        "gm_bs": gain(ks[9], (L, GM_GROUPS, GM_CHUNK)),
        "lru_conv_w": nrm(ks[10], (L, LRU_CONV, LRU_W), LRU_CONV ** -0.5),
        "lru_conv_b": nrm(ks[11], (L, LRU_W), 0.01),
        "lru_wr": nrm(ks[13], (L, LRU_BLOCKS, LRU_W // LRU_BLOCKS, LRU_W // LRU_BLOCKS), (LRU_W // LRU_BLOCKS) ** -0.5),
        "lru_br": nrm(ks[14], (L, LRU_W), 0.01),
        "lru_wi": nrm(ks[15], (L, LRU_BLOCKS, LRU_W // LRU_BLOCKS, LRU_W // LRU_BLOCKS), (LRU_W // LRU_BLOCKS) ** -0.5),
        "lru_bi": nrm(ks[16], (L, LRU_W), 0.01),
        "lru_lambda": lam,
        "swa_q_gain": gain(ks[17], (L, HEAD_DIM)),
        "swa_k_gain": gain(ks[18], (L, HEAD_DIM)),
        "swa_sinks": nrm(ks[19], (L, SWA_HEADS), 0.5),
        "w_mem_kv": nrm(ks[20], (L, D_MODEL, 2 * XA_HEADS * XA_DIM), D_MODEL ** -0.5),
        "xa_q_gain": gain(ks[21], (L, XA_DIM)),
        "xa_k_gain": gain(ks[22], (L, XA_DIM)),
        "w_branch": nrm(ks[23], (L, N_BRANCH, BRANCH_W, D_MODEL), BRANCH_W ** -0.5),
        "w_out": nrm(ks[24], (L, D_MODEL, D_MODEL), D_MODEL ** -0.5),
        "w_ff1": nrm(ks[25], (L, D_MODEL, D_FF), D_MODEL ** -0.5),
        "w_ff2": nrm(ks[26], (L, D_FF, D_MODEL), D_FF ** -0.5),
    }


def reference(x, mem, norm_mix, norm_mem, norm_mlp, w_in, b_gate, gm_v_gain, gm_ws, gm_bs,
              lru_conv_w, lru_conv_b, lru_wr, lru_br, lru_wi, lru_bi, lru_lambda,
              swa_q_gain, swa_k_gain, swa_sinks, w_mem_kv, xa_q_gain, xa_k_gain,
              w_branch, w_out, w_ff1, w_ff2):
    B, S, D = x.shape
    cos, sin = rope_tables(S)
    for l in range(DEPTH):
        h = rmsnorm(x, norm_mix[l])
        z = h @ w_in[l]
        z_gm, z_lru, q_s, k_s, v_s, q_x, z_gate = jnp.split(z, IN_SPLITS, axis=-1)
        mem_n = rmsnorm(mem, norm_mem[l])
        o_gm = gmlp_branch(z_gm, gm_v_gain[l], gm_ws[l], gm_bs[l])
        o_lru = rglru_branch(z_lru, lru_conv_w[l], lru_conv_b[l], lru_wr[l], lru_br[l],
                             lru_wi[l], lru_bi[l], lru_lambda[l])
        o_swa = swa_branch(q_s, k_s, v_s, swa_q_gain[l], swa_k_gain[l], swa_sinks[l], cos, sin)
        o_xa = cross_branch(q_x, mem_n, w_mem_kv[l], xa_q_gain[l], xa_k_gain[l])
        o = jnp.stack([o_gm, o_lru, o_swa, o_xa], axis=2)
        p = jnp.einsum('bsnc,ncd->bsnd', o, w_branch[l])
        g = jax.nn.sigmoid(z_gate.reshape(B, S, N_BRANCH, D) + b_gate[l])
        x = x + jnp.sum(g * p, axis=2) @ w_out[l]
        h = rmsnorm(x, norm_mlp[l])
        x = x + jnp.square(jax.nn.relu(h @ w_ff1[l])) @ w_ff2[l]
    return x
```

```python
import functools

import numpy as np
import jax
import jax.numpy as jnp
from jax import lax
from jax.experimental import pallas as pl
from jax.experimental.pallas import tpu as pltpu

D_MODEL = 1024
DEPTH = 4
N_MEM = 256
EPS = 1e-6
N_BRANCH = 4
BRANCH_W = 512
GM_GROUPS = 4
GM_CHUNK = 128
LRU_BLOCKS = 8
LRU_CONV = 4
LRU_C = 8.0
HEAD_DIM = 64
SWA_HEADS = 8
SWA_KV = 2
WINDOW = 128
ROPE_THETA = 500000.0
ROT_DIM = 16
XA_HEADS = 4
XA_DIM = 128
D_FF = 4 * D_MODEL

OFF_GM = 0
OFF_LRU = 1024
OFF_QKV = 2048
OFF_QX = 2816
OFF_GATE = 3328
D_IN = 7424

LANES = 128
SUBLANES = 8
T_MIX = 256
T_FFN = 512
VMEM_LIMIT = 56 * 1024 * 1024

F32 = jnp.float32
BF16 = jnp.bfloat16


def _dot(a, b):
    return jnp.dot(a, b, preferred_element_type=F32)


def _rms(x, gain_row, n):
    ms = jnp.sum(x * x, axis=-1, keepdims=True) * (1.0 / n)
    return x * lax.rsqrt(ms + EPS) * gain_row


def _mem_kernel(mem_ref, nmem_ref, wkv_ref, kg_ref, kt_ref, vx_ref):
    mn = _rms(mem_ref[0], nmem_ref[0], D_MODEL).astype(BF16)
    kv = _dot(mn, wkv_ref[0])
    ones = jnp.ones((N_MEM, XA_DIM), BF16)
    for hd in range(XA_HEADS):
        kh = _rms(kv[:, hd * XA_DIM:(hd + 1) * XA_DIM], kg_ref[0], XA_DIM)
        kt_ref[0, 0, hd] = kh.T.astype(BF16)
        vh = kv[:, BRANCH_W + hd * XA_DIM:BRANCH_W + (hd + 1) * XA_DIM].astype(BF16)
        vx_ref[0, 0, hd] = jnp.concatenate([vh, ones], axis=1)


def _mem_kv(mem, norm_mem, w_mem_kv_bf, xa_k_gain):
    L, B = DEPTH, mem.shape[0]
    return pl.pallas_call(
        _mem_kernel,
        grid=(L, B),
        in_specs=[
            pl.BlockSpec((1, N_MEM, D_MODEL), lambda l, b: (b, 0, 0)),
            pl.BlockSpec((1, 1, D_MODEL), lambda l, b: (l, 0, 0)),
            pl.BlockSpec((1, D_MODEL, 2 * BRANCH_W), lambda l, b: (l, 0, 0)),
            pl.BlockSpec((1, 1, XA_DIM), lambda l, b: (l, 0, 0)),
        ],
        out_specs=[
            pl.BlockSpec((1, 1, XA_HEADS, XA_DIM, N_MEM), lambda l, b: (l, b, 0, 0, 0)),
            pl.BlockSpec((1, 1, XA_HEADS, N_MEM, 2 * XA_DIM), lambda l, b: (l, b, 0, 0, 0)),
        ],
        out_shape=[
            jax.ShapeDtypeStruct((L, B, XA_HEADS, XA_DIM, N_MEM), BF16),
            jax.ShapeDtypeStruct((L, B, XA_HEADS, N_MEM, 2 * XA_DIM), BF16),
        ],
        compiler_params=pltpu.CompilerParams(
            dimension_semantics=("arbitrary", "arbitrary")),
        name="mem_kv",
    )(mem, norm_mem.reshape(L, 1, D_MODEL), w_mem_kv_bf, xa_k_gain.reshape(L, 1, XA_DIM))


def _mixer_kernel(x_ref, cos_ref, sa_ref, sb_ref, kt_ref, vx_ref,
                  w_in_ref, w_br_ref, w_out_ref,
                  nmix_ref, gmv_ref, gmws_ref, gmb_ref,
                  cw_ref, cb_ref, wri_ref, br_ref, bi_ref, lam_ref,
                  qkg_ref, ones_ref, sink_ref, xag_ref, bg_ref,
                  o_ref,
                  kprev, vprev, xtail, hcar, xbuf, a_s, b_s, hl_s, ac_s, *, T):
    t = pl.program_id(1)
    nw = T // WINDOW
    J = T // SUBLANES
    P = J + SUBLANES
    NS = BRANCH_W // LANES

    cur = lax.rem(t, 2)
    nxt = 1 - cur

    @pl.when(t == 0)
    def _():
        kprev[0] = jnp.zeros((WINDOW, LANES), BF16)
        vprev[0] = jnp.zeros((WINDOW, 2 * LANES), BF16)
        xtail[0] = jnp.zeros((SUBLANES, BRANCH_W), F32)
        hcar[0] = jnp.zeros((SUBLANES, BRANCH_W), F32)

    x = x_ref[0]
    hb = _rms(x, nmix_ref[...], D_MODEL).astype(BF16)

    def proj(lo, hi):
        return _dot(hb, w_in_ref[:, lo:hi])

    zg = jax.nn.gelu(proj(OFF_GM, OFF_GM + 2 * BRANCH_W))
    u = zg[:, :BRANCH_W]
    vn = _rms(zg[:, BRANCH_W:], gmv_ref[...], BRANCH_W).astype(BF16)
    r_i = lax.broadcasted_iota(jnp.int32, (GM_CHUNK, GM_CHUNK), 0)
    c_i = lax.broadcasted_iota(jnp.int32, (GM_CHUNK, GM_CHUNK), 1)
    causal = r_i >= c_i
    s_cols = []
    for g in range(GM_GROUPS):
        wg = jnp.where(causal, gmws_ref[g], 0.0).astype(BF16)
        vcat = jnp.concatenate(
            [vn[c * GM_CHUNK:(c + 1) * GM_CHUNK, g * LANES:(g + 1) * LANES] for c in range(nw)], axis=1)
        sg = _dot(wg, vcat)
        s_cols.append(jnp.concatenate(
            [sg[:, c * LANES:(c + 1) * LANES] for c in range(nw)], axis=0))
    s_full = jnp.concatenate(s_cols, axis=1) + jnp.concatenate([gmb_ref[...]] * nw, axis=0)
    o_gm = (u * s_full).astype(BF16)

    zl = proj(OFF_LRU, OFF_LRU + 2 * BRANCH_W)
    xb = zl[:, :BRANCH_W]
    gb = zl[:, BRANCH_W:]
    xbuf[0:SUBLANES, :] = xtail[cur]
    xbuf[SUBLANES:SUBLANES + T, :] = xb
    xtail[nxt] = xb[T - SUBLANES:, :]
    xc = cb_ref[...] + xb * cw_ref[3:4, :]
    for k in range(LRU_CONV - 1):
        sh = LRU_CONV - 1 - k
        xc = xc + xbuf[SUBLANES - sh:SUBLANES - sh + T, :] * cw_ref[k:k + 1, :]
    ri = _dot(xc.astype(BF16), wri_ref[...])
    r_gate = jax.nn.sigmoid(ri[:, :BRANCH_W] + br_ref[...])
    i_gate = jax.nn.sigmoid(ri[:, BRANCH_W:] + bi_ref[...])
    nl = -lam_ref[...]
    softplus = jnp.maximum(nl, 0.0) + jnp.log1p(jnp.exp(-jnp.abs(nl)))
    log_a = (-LRU_C) * r_gate * softplus
    a = jnp.exp(log_a)
    b_in = xc * i_gate * jnp.sqrt(1.0 - a * a)
    for s in range(SUBLANES):
        for sl in range(NS):
            a_s[sl, s * P:s * P + J, :] = a[s * J:(s + 1) * J, sl * LANES:(sl + 1) * LANES]
            b_s[sl, s * P:s * P + J, :] = b_in[s * J:(s + 1) * J, sl * LANES:(sl + 1) * LANES]

    def scan_body(j, carry):
        new = []
        for sl in range(NS):
            hl, ac = carry[2 * sl], carry[2 * sl + 1]
            aj = a_s[sl, pl.ds(j, SUBLANES, stride=P), :]
            bj = b_s[sl, pl.ds(j, SUBLANES, stride=P), :]
            hl = aj * hl + bj
            ac = aj * ac
            hl_s[sl, pl.ds(j, SUBLANES, stride=P), :] = hl
            ac_s[sl, pl.ds(j, SUBLANES, stride=P), :] = ac
            new += [hl, ac]
        return tuple(new)

    init = tuple(jnp.zeros((SUBLANES, LANES), F32) if i % 2 == 0 else jnp.ones((SUBLANES, LANES), F32)
                 for i in range(2 * NS))
    fin = lax.fori_loop(0, J, scan_body, init)
    h_cols = []
    for sl in range(NS):
        hl_end, ac_end = fin[2 * sl], fin[2 * sl + 1]
        hin = hcar[cur, 0:1, sl * LANES:(sl + 1) * LANES]
        segs = []
        for s in range(SUBLANES):
            segs.append(hl_s[sl, s * P:s * P + J, :] + ac_s[sl, s * P:s * P + J, :] * hin)
            hin = hl_end[s:s + 1, :] + ac_end[s:s + 1, :] * hin
        hcar[nxt, 0:1, sl * LANES:(sl + 1) * LANES] = hin
        h_cols.append(jnp.concatenate(segs, axis=0))
    h_lru = jnp.concatenate(h_cols, axis=1)
    o_lru = (jax.nn.gelu(gb) * h_lru).astype(BF16)

    zq = proj(OFF_QKV, OFF_QKV + 768)
    qk = zq[:, :640]
    vv = zq[:, 640:768]
    ssq = _dot((qk * qk).astype(BF16), ones_ref[...])
    qkn = qk * lax.rsqrt(ssq * (1.0 / HEAD_DIM) + EPS) * qkg_ref[...]
    cos_t, sa_t, sb_t = cos_ref[...], sa_ref[...], sb_ref[...]
    lane = lax.broadcasted_iota(jnp.int32, (T, LANES), 1)
    low_half = lane < HEAD_DIM
    roped = []
    for p in range(5):
        xt = qkn[:, p * LANES:(p + 1) * LANES]
        roped.append(xt * cos_t + pltpu.roll(xt, LANES - ROT_DIM // 2, 1) * sa_t
                     + pltpu.roll(xt, ROT_DIM // 2, 1) * sb_t)
    q_lo = [jnp.where(low_half, roped[p], 0.0).astype(BF16) for p in range(4)]
    q_hi = [jnp.where(low_half, 0.0, roped[p]).astype(BF16) for p in range(4)]
    k_r = roped[4].astype(BF16)
    v_x = jnp.concatenate([vv.astype(BF16), jnp.ones((T, LANES), BF16)], axis=1)
    kext = jnp.concatenate([kprev[cur], k_r], axis=0)
    vext = jnp.concatenate([vprev[cur], v_x], axis=0)
    kprev[nxt] = k_r[T - WINDOW:, :]
    vprev[nxt] = v_x[T - WINDOW:, :]
    qi = lax.broadcasted_iota(jnp.int32, (WINDOW, 2 * WINDOW), 0)
    kj = lax.broadcasted_iota(jnp.int32, (WINDOW, 2 * WINDOW), 1)
    diff = qi + WINDOW - kj
    band = jnp.where(diff >= 0, diff, WINDOW) < WINDOW
    bias = jnp.where(band, 0.0, -jnp.inf)
    kj0 = kj + jnp.where(t > 0, WINDOW, 0)
    bias0 = jnp.where(kj0 >= WINDOW, bias, -jnp.inf)
    sink = sink_ref[...]
    lane_w = lax.broadcasted_iota(jnp.int32, (WINDOW, LANES), 1) < HEAD_DIM
    o_win = []
    for n in range(nw):
        rows = slice(n * WINDOW, (n + 1) * WINDOW)
        qs = jnp.concatenate([q[rows] for p in range(4) for q in (q_lo[p], q_hi[p])], axis=0)
        kw = kext[n * WINDOW:(n + 2) * WINDOW]
        s = lax.dot_general(qs, kw, (((1,), (1,)), ((), ())), preferred_element_type=F32)
        s = s + jnp.concatenate([bias0 if n == 0 else bias] * SWA_HEADS, axis=0)
        m = jnp.maximum(jnp.max(s, axis=-1, keepdims=True), sink)
        pr = jnp.exp(s - m).astype(BF16)
        oe = _dot(pr, vext[n * WINDOW:(n + 2) * WINDOW])
        on = oe[:, :LANES] / (oe[:, LANES:] + jnp.exp(sink - m))
        o_win.append(jnp.concatenate(
            [jnp.where(lane_w, on[(2 * p) * WINDOW:(2 * p + 1) * WINDOW],
                       on[(2 * p + 1) * WINDOW:(2 * p + 2) * WINDOW]) for p in range(4)], axis=1))
    o_swa = jnp.concatenate(o_win, axis=0).astype(BF16)

    zx = proj(OFF_QX, OFF_QX + BRANCH_W)
    xg = xag_ref[...] * (XA_DIM ** -0.5)
    o_heads = []
    for hd in range(XA_HEADS):
        qn = _rms(zx[:, hd * XA_DIM:(hd + 1) * XA_DIM], xg, XA_DIM).astype(BF16)
        s = _dot(qn, kt_ref[0, hd])
        pr = jnp.exp(s - jnp.max(s, axis=-1, keepdims=True)).astype(BF16)
        oe = _dot(pr, vx_ref[0, hd])
        o_heads.append(oe[:, :XA_DIM] / oe[:, XA_DIM:])
    o_xa = jnp.concatenate(o_heads, axis=1).astype(BF16)

    acc = None
    for b, ob in enumerate((o_gm, o_lru, o_swa, o_xa)):
        gate = jax.nn.sigmoid(proj(OFF_GATE + b * D_MODEL, OFF_GATE + (b + 1) * D_MODEL) + bg_ref[b:b + 1, :])
        contrib = gate * _dot(ob, w_br_ref[b])
        acc = contrib if acc is None else acc + contrib
    o_ref[0] = x + _dot(acc.astype(BF16), w_out_ref[...])


def _mixer(x, rope, kt, vx, w_in, w_br, w_out, small):
    B, S, D = x.shape
    T = T_MIX
    J = T // SUBLANES
    P = J + SUBLANES
    NS = BRANCH_W // LANES
    vmem = pl.BlockSpec(memory_space=pltpu.VMEM)
    tok = pl.BlockSpec((1, T, D), lambda b, t: (b, t, 0))
    rp = pl.BlockSpec((T, LANES), lambda b, t: (t, 0))
    return pl.pallas_call(
        functools.partial(_mixer_kernel, T=T),
        grid=(B, S // T),
        in_specs=[tok, rp, rp, rp,
                  pl.BlockSpec((1, XA_HEADS, XA_DIM, N_MEM), lambda b, t: (b, 0, 0, 0)),
                  pl.BlockSpec((1, XA_HEADS, N_MEM, 2 * XA_DIM), lambda b, t: (b, 0, 0, 0)),
                  vmem, vmem, vmem] + [vmem] * len(small),
        out_specs=tok,
        out_shape=jax.ShapeDtypeStruct((B, S, D), F32),
        scratch_shapes=[
            pltpu.VMEM((2, WINDOW, LANES), BF16),
            pltpu.VMEM((2, WINDOW, 2 * LANES), BF16),
            pltpu.VMEM((2, SUBLANES, BRANCH_W), F32),
            pltpu.VMEM((2, SUBLANES, BRANCH_W), F32),
            pltpu.VMEM((T + SUBLANES, BRANCH_W), F32),
            pltpu.VMEM((NS, SUBLANES * P, LANES), F32),
            pltpu.VMEM((NS, SUBLANES * P, LANES), F32),
            pltpu.VMEM((NS, SUBLANES * P, LANES), F32),
            pltpu.VMEM((NS, SUBLANES * P, LANES), F32),
        ],
        compiler_params=pltpu.CompilerParams(
            dimension_semantics=("arbitrary", "arbitrary"),
            vmem_limit_bytes=VMEM_LIMIT),
        name="mixer",
    )(x, *rope, kt, vx, w_in, w_br, w_out, *small)


def _ffn_kernel(x_ref, g_ref, w1_ref, w2_ref, o_ref):
    x = x_ref[...]
    hb = _rms(x, g_ref[...], D_MODEL).astype(BF16)
    acc = x
    for c in range(D_FF // D_MODEL):
        hid = _dot(hb, w1_ref[:, c * D_MODEL:(c + 1) * D_MODEL])
        hid = jnp.square(jnp.maximum(hid, 0.0)).astype(BF16)
        acc = acc + _dot(hid, w2_ref[c * D_MODEL:(c + 1) * D_MODEL, :])
    o_ref[...] = acc


def _ffn(x2, g, w1, w2):
    N, D = x2.shape
    vmem = pl.BlockSpec(memory_space=pltpu.VMEM)
    tok = pl.BlockSpec((T_FFN, D), lambda i: (i, 0))
    return pl.pallas_call(
        _ffn_kernel,
        grid=(N // T_FFN,),
        in_specs=[tok, vmem, vmem, vmem],
        out_specs=tok,
        out_shape=jax.ShapeDtypeStruct((N, D), F32),
        compiler_params=pltpu.CompilerParams(
            dimension_semantics=("arbitrary",), vmem_limit_bytes=VMEM_LIMIT),
        name="ffn",
    )(x2, g, w1, w2)


def _rope_tables(seq):
    pos = jnp.arange(seq, dtype=F32)
    half = ROT_DIM // 2
    inv = ROPE_THETA ** (-jnp.arange(0, ROT_DIM, 2, dtype=F32) / ROT_DIM)
    ang = pos[:, None] * inv[None, :]
    cos, sin = jnp.cos(ang), jnp.sin(ang)
    pad = jnp.zeros((seq, HEAD_DIM - ROT_DIM), F32)
    zero = jnp.zeros((seq, half), F32)
    c64 = jnp.concatenate([cos, cos, pad + 1.0], axis=1)
    a64 = jnp.concatenate([-sin, zero, pad], axis=1)
    b64 = jnp.concatenate([zero, sin, pad], axis=1)
    return tuple(jnp.concatenate([tbl, tbl], axis=1) for tbl in (c64, a64, b64))


def _pair_heads(w, axis):
    shp = w.shape
    w = w.reshape(shp[:axis] + (SWA_KV, SWA_HEADS // SWA_KV, HEAD_DIM) + shp[axis + 1:])
    w = jnp.swapaxes(w, axis, axis + 1)
    return w.reshape(shp)


@jax.jit
def kernel(x, mem, norm_mix, norm_mem, norm_mlp, w_in, b_gate, gm_v_gain, gm_ws, gm_bs, lru_conv_w, lru_conv_b, lru_wr, lru_br, lru_wi, lru_bi, lru_lambda, swa_q_gain, swa_k_gain, swa_sinks, w_mem_kv, xa_q_gain, xa_k_gain, w_branch, w_out, w_ff1, w_ff2):
    B, S, D = x.shape
    L = DEPTH
    G = SWA_HEADS // SWA_KV

    q_cols = _pair_heads(w_in[:, :, OFF_QKV:OFF_QKV + BRANCH_W], 2)
    w_in_bf = jnp.concatenate(
        [w_in[:, :, :OFF_QKV], q_cols, w_in[:, :, OFF_QKV + BRANCH_W:]], axis=2).astype(BF16)
    w_br_bf = jnp.concatenate(
        [w_branch[:, :2], _pair_heads(w_branch[:, 2], 1)[:, None], w_branch[:, 3:]], axis=1).astype(BF16)
    w_out_bf = w_out.astype(BF16)
    w1_bf = w_ff1.astype(BF16)
    w2_bf = w_ff2.astype(BF16)
    eye = jnp.eye(LRU_BLOCKS, dtype=F32)
    bd = lambda w: (eye[None, :, None, :, None] * w[:, :, :, None, :]).reshape(L, BRANCH_W, BRANCH_W)
    wri_bf = jnp.concatenate([bd(lru_wr), bd(lru_wi)], axis=2).astype(BF16)
    gm_bias = jnp.broadcast_to(
        jnp.swapaxes(gm_bs, 1, 2)[:, :, :, None], (L, GM_CHUNK, GM_GROUPS, LANES)).reshape(L, GM_CHUNK, BRANCH_W)
    qk_gain = jnp.concatenate(
        [jnp.tile(swa_q_gain, (1, SWA_HEADS)) * (HEAD_DIM ** -0.5), jnp.tile(swa_k_gain, (1, SWA_KV))], axis=1)
    sink_rows = jnp.repeat(
        jnp.swapaxes(swa_sinks.reshape(L, SWA_KV, G), 1, 2).reshape(L, SWA_HEADS), WINDOW, axis=1)[:, :, None]
    ones_bd = jnp.asarray(np.kron(np.eye(10, dtype=np.float32), np.ones((HEAD_DIM, HEAD_DIM), np.float32)), BF16)
    rope = _rope_tables(S)

    kt, vx = _mem_kv(mem, norm_mem, w_mem_kv.astype(BF16), xa_k_gain)

    row = lambda a, l: a[l][None, :]
    for l in range(L):
        small = (row(norm_mix, l), row(gm_v_gain, l), gm_ws[l], gm_bias[l],
                 lru_conv_w[l], row(lru_conv_b, l), wri_bf[l], row(lru_br, l), row(lru_bi, l),
                 row(lru_lambda, l), qk_gain[l][None, :], ones_bd, sink_rows[l],
                 row(xa_q_gain, l), b_gate[l])
        x = _mixer(x, rope, kt[l], vx[l], w_in_bf[l], w_br_bf[l], w_out_bf[l], small)
        x = _ffn(x.reshape(B * S, D), row(norm_mlp, l), w1_bf[l], w2_bf[l]).reshape(B, S, D)
    return x
```

```python
import functools

import numpy as np
import jax
import jax.numpy as jnp
from jax import lax
from jax.experimental import pallas as pl
from jax.experimental.pallas import tpu as pltpu

D_MODEL = 1024
DEPTH = 4
N_MEM = 256
EPS = 1e-6
N_BRANCH = 4
BRANCH_W = 512
GM_GROUPS = 4
GM_CHUNK = 128
LRU_BLOCKS = 8
LRU_CONV = 4
LRU_C = 8.0
HEAD_DIM = 64
SWA_HEADS = 8
SWA_KV = 2
WINDOW = 128
ROPE_THETA = 500000.0
ROT_DIM = 16
XA_HEADS = 4
XA_DIM = 128
D_FF = 4 * D_MODEL

OFF_GM = 0
OFF_LRU = 1024
OFF_QKV = 2048
OFF_QX = 2816
OFF_GATE = 3328
D_IN = 7424

LANES = 128
SUBLANES = 8
T_MIX = 256
T_FFN = 512
VMEM_LIMIT = 56 * 1024 * 1024

F32 = jnp.float32
BF16 = jnp.bfloat16


def _dot(a, b):
    return jnp.dot(a, b, preferred_element_type=F32)


def _rms(x, gain_row, n):
    ms = jnp.sum(x * x, axis=-1, keepdims=True) * (1.0 / n)
    return x * lax.rsqrt(ms + EPS) * gain_row


def _layer_spec(shape, l):
    nd = len(shape)
    return pl.BlockSpec((None,) + tuple(shape[1:]), lambda *_: (l,) + (0,) * (nd - 1),
                        pipeline_mode=pl.Buffered(1))


def _mem_kernel(mem_ref, nmem_ref, wkv_ref, kg_ref, kt_ref, vx_ref):
    mn = _rms(mem_ref[0], nmem_ref[0], D_MODEL).astype(BF16)
    kv = _dot(mn, wkv_ref[0])
    ones = jnp.ones((N_MEM, XA_DIM), BF16)
    for hd in range(XA_HEADS):
        kh = _rms(kv[:, hd * XA_DIM:(hd + 1) * XA_DIM], kg_ref[0], XA_DIM)
        kt_ref[0, 0, hd] = kh.T.astype(BF16)
        vh = kv[:, BRANCH_W + hd * XA_DIM:BRANCH_W + (hd + 1) * XA_DIM].astype(BF16)
        vx_ref[0, 0, hd] = jnp.concatenate([vh, ones], axis=1)


def _mem_kv(mem, norm_mem, w_mem_kv_bf, xa_k_gain):
    L, B = DEPTH, mem.shape[0]
    return pl.pallas_call(
        _mem_kernel,
        grid=(L, B),
        in_specs=[
            pl.BlockSpec((1, N_MEM, D_MODEL), lambda l, b: (b, 0, 0)),
            pl.BlockSpec((1, 1, D_MODEL), lambda l, b: (l, 0, 0)),
            pl.BlockSpec((1, D_MODEL, 2 * BRANCH_W), lambda l, b: (l, 0, 0)),
            pl.BlockSpec((1, 1, XA_DIM), lambda l, b: (l, 0, 0)),
        ],
        out_specs=[
            pl.BlockSpec((1, 1, XA_HEADS, XA_DIM, N_MEM), lambda l, b: (l, b, 0, 0, 0)),
            pl.BlockSpec((1, 1, XA_HEADS, N_MEM, 2 * XA_DIM), lambda l, b: (l, b, 0, 0, 0)),
        ],
        out_shape=[
            jax.ShapeDtypeStruct((L, B, XA_HEADS, XA_DIM, N_MEM), BF16),
            jax.ShapeDtypeStruct((L, B, XA_HEADS, N_MEM, 2 * XA_DIM), BF16),
        ],
        compiler_params=pltpu.CompilerParams(
            dimension_semantics=("arbitrary", "arbitrary")),
        name="mem_kv",
    )(mem, norm_mem.reshape(L, 1, D_MODEL), w_mem_kv_bf, xa_k_gain.reshape(L, 1, XA_DIM))


def _mixer_kernel(x_ref, cos_ref, sa_ref, sb_ref, kt_ref, vx_ref,
                  w_in_ref, w_br_ref, w_out_ref,
                  nmix_ref, gmv_ref, gmws_ref, gmb_ref,
                  cw_ref, cb_ref, wri_ref, br_ref, bi_ref, lam_ref,
                  qkg_ref, ones_ref, sink_ref, xag_ref, bg_ref,
                  o_ref,
                  kprev, vprev, xtail, hcar, xbuf, a_s, b_s, hl_s, ac_s, *, T):
    t = pl.program_id(1)
    nw = T // WINDOW
    J = T // SUBLANES
    P = J + SUBLANES
    NS = BRANCH_W // LANES

    cur = lax.rem(t, 2)
    nxt = 1 - cur

    @pl.when(t == 0)
    def _():
        kprev[0] = jnp.zeros((WINDOW, 2 * LANES), BF16)
        vprev[0] = jnp.zeros((WINDOW, 3 * LANES), BF16)
        xtail[0] = jnp.zeros((SUBLANES, BRANCH_W), F32)
        hcar[0] = jnp.zeros((SUBLANES, BRANCH_W), F32)

    x = x_ref[0]
    hb = _rms(x, nmix_ref[...], D_MODEL).astype(BF16)

    def proj(lo, hi):
        return _dot(hb, w_in_ref[:, lo:hi])

    zg = jax.nn.gelu(proj(OFF_GM, OFF_GM + 2 * BRANCH_W))
    u = zg[:, :BRANCH_W]
    vn = _rms(zg[:, BRANCH_W:], gmv_ref[...], BRANCH_W).astype(BF16)
    r_i = lax.broadcasted_iota(jnp.int32, (GM_CHUNK, GM_CHUNK), 0)
    c_i = lax.broadcasted_iota(jnp.int32, (GM_CHUNK, GM_CHUNK), 1)
    causal = r_i >= c_i
    s_cols = []
    for g in range(GM_GROUPS):
        wg = jnp.where(causal, gmws_ref[g], 0.0).astype(BF16)
        vcat = jnp.concatenate(
            [vn[c * GM_CHUNK:(c + 1) * GM_CHUNK, g * LANES:(g + 1) * LANES] for c in range(nw)], axis=1)
        sg = _dot(wg, vcat)
        s_cols.append(jnp.concatenate(
            [sg[:, c * LANES:(c + 1) * LANES] for c in range(nw)], axis=0))
    s_full = jnp.concatenate(s_cols, axis=1) + jnp.concatenate([gmb_ref[...]] * nw, axis=0)
    o_gm = (u * s_full).astype(BF16)

    zl = proj(OFF_LRU, OFF_LRU + 2 * BRANCH_W)
    xb = zl[:, :BRANCH_W]
    gb = zl[:, BRANCH_W:]
    xbuf[0:SUBLANES, :] = xtail[cur]
    xbuf[SUBLANES:SUBLANES + T, :] = xb
    xtail[nxt] = xb[T - SUBLANES:, :]
    xc = cb_ref[...] + xb * cw_ref[3:4, :]
    for k in range(LRU_CONV - 1):
        sh = LRU_CONV - 1 - k
        xc = xc + xbuf[SUBLANES - sh:SUBLANES - sh + T, :] * cw_ref[k:k + 1, :]
    ri = _dot(xc.astype(BF16), wri_ref[...])
    r_gate = jax.nn.sigmoid(ri[:, :BRANCH_W] + br_ref[...])
    i_gate = jax.nn.sigmoid(ri[:, BRANCH_W:] + bi_ref[...])
    nl = -lam_ref[...]
    softplus = jnp.maximum(nl, 0.0) + jnp.log1p(jnp.exp(-jnp.abs(nl)))
    log_a = (-LRU_C) * r_gate * softplus
    a = jnp.exp(log_a)
    b_in = xc * i_gate * jnp.sqrt(1.0 - a * a)
    for s in range(SUBLANES):
        for sl in range(NS):
            a_s[sl, s * P:s * P + J, :] = a[s * J:(s + 1) * J, sl * LANES:(sl + 1) * LANES]
            b_s[sl, s * P:s * P + J, :] = b_in[s * J:(s + 1) * J, sl * LANES:(sl + 1) * LANES]

    def scan_body(j, carry):
        new = []
        for sl in range(NS):
            hl, ac = carry[2 * sl], carry[2 * sl + 1]
            aj = a_s[sl, pl.ds(j, SUBLANES, stride=P), :]
            bj = b_s[sl, pl.ds(j, SUBLANES, stride=P), :]
            hl = aj * hl + bj
            ac = aj * ac
            hl_s[sl, pl.ds(j, SUBLANES, stride=P), :] = hl
            ac_s[sl, pl.ds(j, SUBLANES, stride=P), :] = ac
            new += [hl, ac]
        return tuple(new)

    init = tuple(jnp.zeros((SUBLANES, LANES), F32) if i % 2 == 0 else jnp.ones((SUBLANES, LANES), F32)
                 for i in range(2 * NS))
    fin = lax.fori_loop(0, J, scan_body, init)
    h_cols = []
    for sl in range(NS):
        hl_end, ac_end = fin[2 * sl], fin[2 * sl + 1]
        hin = hcar[cur, 0:1, sl * LANES:(sl + 1) * LANES]
        segs = []
        for s in range(SUBLANES):
            segs.append(hl_s[sl, s * P:s * P + J, :] + ac_s[sl, s * P:s * P + J, :] * hin)
            hin = hl_end[s:s + 1, :] + ac_end[s:s + 1, :] * hin
        hcar[nxt, 0:1, sl * LANES:(sl + 1) * LANES] = hin
        h_cols.append(jnp.concatenate(segs, axis=0))
    h_lru = jnp.concatenate(h_cols, axis=1)
    o_lru = (jax.nn.gelu(gb) * h_lru).astype(BF16)

    zq = proj(OFF_QKV, OFF_QKV + 768)
    vv = zq[:, 640:768]
    cos_t, sa_t, sb_t = cos_ref[...], sa_ref[...], sb_ref[...]
    roped = []
    for p in range(5):
        xt = zq[:, p * LANES:(p + 1) * LANES]
        ssq = _dot((xt * xt).astype(BF16), ones_ref[...])
        xt = xt * lax.rsqrt(ssq * (1.0 / HEAD_DIM) + EPS) * qkg_ref[:, p * LANES:(p + 1) * LANES]
        roped.append(xt * cos_t + pltpu.roll(xt, LANES - ROT_DIM // 2, 1) * sa_t
                     + pltpu.roll(xt, ROT_DIM // 2, 1) * sb_t)
    low_half = lax.broadcasted_iota(jnp.int32, (T, LANES), 1) < HEAD_DIM
    q_lo = [jnp.where(low_half, roped[p], 0.0).astype(BF16) for p in range(4)]
    q_hi = [jnp.where(low_half, 0.0, roped[p]).astype(BF16) for p in range(4)]
    half = SWA_HEADS // SWA_KV // 2
    q_same = q_lo[:half] + q_hi[half:]
    q_swap = q_hi[:half] + q_lo[half:]
    ones_t = jnp.ones((T, LANES), BF16)
    k2 = jnp.concatenate([roped[4].astype(BF16),
                          pltpu.roll(roped[4], HEAD_DIM, 1).astype(BF16)], axis=1)
    v3 = jnp.concatenate([vv.astype(BF16), ones_t,
                          pltpu.roll(vv, HEAD_DIM, 1).astype(BF16)], axis=1)
    kext = jnp.concatenate([kprev[cur], k2], axis=0)
    vext = jnp.concatenate([vprev[cur], v3], axis=0)
    kprev[nxt] = k2[T - WINDOW:, :]
    vprev[nxt] = v3[T - WINDOW:, :]
    qi = lax.broadcasted_iota(jnp.int32, (WINDOW, 2 * WINDOW), 0)
    kj = lax.broadcasted_iota(jnp.int32, (WINDOW, 2 * WINDOW), 1)
    diff = qi + WINDOW - kj
    band = jnp.where(diff >= 0, diff, WINDOW) < WINDOW
    bias = jnp.where(band, 0.0, -jnp.inf)
    kj0 = kj + jnp.where(t > 0, WINDOW, 0)
    bias0 = jnp.where(kj0 >= WINDOW, bias, -jnp.inf)
    lane_w = lax.broadcasted_iota(jnp.int32, (WINDOW, LANES), 1) < HEAD_DIM
    nslot = SWA_HEADS // 2

    def attend(qs, kw, vw, sink, bias_w):
        s = lax.dot_general(qs, kw, (((1,), (1,)), ((), ())), preferred_element_type=F32)
        s = s + jnp.concatenate([bias_w] * nslot, axis=0)
        m = jnp.maximum(jnp.max(s, axis=-1, keepdims=True), sink)
        pr = jnp.exp(s - m).astype(BF16)
        oe = _dot(pr, vw)
        return oe[:, :LANES] / (oe[:, LANES:] + jnp.exp(sink - m))

    o_win = []
    for n in range(nw):
        rows = slice(n * WINDOW, (n + 1) * WINDOW)
        krows = slice(n * WINDOW, (n + 2) * WINDOW)
        bias_w = bias0 if n == 0 else bias
        o_same = attend(jnp.concatenate([q[rows] for q in q_same], axis=0),
                        kext[krows, :LANES], vext[krows, :2 * LANES], sink_ref[0], bias_w)
        o_swap = attend(jnp.concatenate([q[rows] for q in q_swap], axis=0),
                        kext[krows, LANES:],
                        jnp.concatenate([vext[krows, 2 * LANES:], vext[krows, LANES:2 * LANES]], axis=1),
                        sink_ref[1], bias_w)
        tiles = []
        for p in range(4):
            blk = slice(p * WINDOW, (p + 1) * WINDOW)
            lo, hi = (o_same[blk], o_swap[blk]) if p < half else (o_swap[blk], o_same[blk])
            tiles.append(jnp.where(lane_w, lo, hi))
        o_win.append(jnp.concatenate(tiles, axis=1))
    o_swa = jnp.concatenate(o_win, axis=0).astype(BF16)

    zx = proj(OFF_QX, OFF_QX + BRANCH_W)
    xg = xag_ref[...] * (XA_DIM ** -0.5)
    o_heads = []
    for hd in range(XA_HEADS):
        qn = _rms(zx[:, hd * XA_DIM:(hd + 1) * XA_DIM], xg, XA_DIM).astype(BF16)
        s = _dot(qn, kt_ref[hd])
        pr = jnp.exp(s - jnp.max(s, axis=-1, keepdims=True)).astype(BF16)
        oe = _dot(pr, vx_ref[hd])
        o_heads.append(oe[:, :XA_DIM] / oe[:, XA_DIM:])
    o_xa = jnp.concatenate(o_heads, axis=1).astype(BF16)

    acc = None
    for b, ob in enumerate((o_gm, o_lru, o_swa, o_xa)):
        gate = jax.nn.sigmoid(proj(OFF_GATE + b * D_MODEL, OFF_GATE + (b + 1) * D_MODEL) + bg_ref[b:b + 1, :])
        contrib = gate * _dot(ob, w_br_ref[b])
        acc = contrib if acc is None else acc + contrib
    o_ref[0] = x + _dot(acc.astype(BF16), w_out_ref[...])


def _mixer(l, x, rope, kt, vx, w_in, w_br, w_out, stacked, shared):
    B, S, D = x.shape
    T = T_MIX
    J = T // SUBLANES
    P = J + SUBLANES
    NS = BRANCH_W // LANES
    vmem = pl.BlockSpec(memory_space=pltpu.VMEM)
    tok = pl.BlockSpec((1, T, D), lambda b, t: (b, t, 0))
    rp = pl.BlockSpec((T, LANES), lambda b, t: (t, 0))
    s = stacked
    operands = [
        (x, tok), (rope[0], rp), (rope[1], rp), (rope[2], rp),
        (kt, pl.BlockSpec((None, None, XA_HEADS, XA_DIM, N_MEM), lambda b, t: (l, b, 0, 0, 0))),
        (vx, pl.BlockSpec((None, None, XA_HEADS, N_MEM, 2 * XA_DIM), lambda b, t: (l, b, 0, 0, 0))),
    ]
    for arr in (w_in, w_br, w_out, s["norm_mix"], s["gm_v_gain"], s["gm_ws"], s["gm_bias"],
                s["conv_w"], s["conv_b"], s["w_ri"], s["lru_br"], s["lru_bi"], s["lam"], s["qk_gain"]):
        operands.append((arr, _layer_spec(arr.shape, l)))
    operands.append((shared["ones_tile"], vmem))
    for arr in (s["sinks"], s["xa_q_gain"], s["b_gate"]):
        operands.append((arr, _layer_spec(arr.shape, l)))
    return pl.pallas_call(
        functools.partial(_mixer_kernel, T=T),
        grid=(B, S // T),
        in_specs=[spec for _, spec in operands],
        out_specs=tok,
        out_shape=jax.ShapeDtypeStruct((B, S, D), F32),
        scratch_shapes=[
            pltpu.VMEM((2, WINDOW, 2 * LANES), BF16),
            pltpu.VMEM((2, WINDOW, 3 * LANES), BF16),
            pltpu.VMEM((2, SUBLANES, BRANCH_W), F32),
            pltpu.VMEM((2, SUBLANES, BRANCH_W), F32),
            pltpu.VMEM((T + SUBLANES, BRANCH_W), F32),
            pltpu.VMEM((NS, SUBLANES * P, LANES), F32),
            pltpu.VMEM((NS, SUBLANES * P, LANES), F32),
            pltpu.VMEM((NS, SUBLANES * P, LANES), F32),
            pltpu.VMEM((NS, SUBLANES * P, LANES), F32),
        ],
        compiler_params=pltpu.CompilerParams(
            dimension_semantics=("arbitrary", "arbitrary"),
            vmem_limit_bytes=VMEM_LIMIT),
        name="mixer",
    )(*[arr for arr, _ in operands])


def _ffn_kernel(x_ref, g_ref, w1_ref, w2_ref, o_ref):
    x = x_ref[...]
    hb = _rms(x, g_ref[...], D_MODEL).astype(BF16)
    acc = x
    for c in range(D_FF // D_MODEL):
        hid = _dot(hb, w1_ref[:, c * D_MODEL:(c + 1) * D_MODEL])
        hid = jnp.square(jnp.maximum(hid, 0.0)).astype(BF16)
        acc = acc + _dot(hid, w2_ref[c * D_MODEL:(c + 1) * D_MODEL, :])
    o_ref[...] = acc


def _ffn(l, x2, g, w1, w2):
    N, D = x2.shape
    tok = pl.BlockSpec((T_FFN, D), lambda i: (i, 0))
    return pl.pallas_call(
        _ffn_kernel,
        grid=(N // T_FFN,),
        in_specs=[tok, _layer_spec(g.shape, l), _layer_spec(w1.shape, l), _layer_spec(w2.shape, l)],
        out_specs=tok,
        out_shape=jax.ShapeDtypeStruct((N, D), F32),
        compiler_params=pltpu.CompilerParams(
            dimension_semantics=("arbitrary",), vmem_limit_bytes=VMEM_LIMIT),
        name="ffn",
    )(x2, g, w1, w2)


def _rope_tables(seq):
    pos = jnp.arange(seq, dtype=F32)
    half = ROT_DIM // 2
    inv = ROPE_THETA ** (-jnp.arange(0, ROT_DIM, 2, dtype=F32) / ROT_DIM)
    ang = pos[:, None] * inv[None, :]
    cos, sin = jnp.cos(ang), jnp.sin(ang)
    pad = jnp.zeros((seq, HEAD_DIM - ROT_DIM), F32)
    zero = jnp.zeros((seq, half), F32)
    c64 = jnp.concatenate([cos, cos, pad + 1.0], axis=1)
    a64 = jnp.concatenate([-sin, zero, pad], axis=1)
    b64 = jnp.concatenate([zero, sin, pad], axis=1)
    return tuple(jnp.concatenate([tbl, tbl], axis=1) for tbl in (c64, a64, b64))


@jax.jit
def kernel(x, mem, norm_mix, norm_mem, norm_mlp, w_in, b_gate, gm_v_gain, gm_ws, gm_bs, lru_conv_w, lru_conv_b, lru_wr, lru_br, lru_wi, lru_bi, lru_lambda, swa_q_gain, swa_k_gain, swa_sinks, w_mem_kv, xa_q_gain, xa_k_gain, w_branch, w_out, w_ff1, w_ff2):
    B, S, D = x.shape
    L = DEPTH
    row = lambda a: a[:, None, :]

    w_in_bf = w_in.astype(BF16)
    w_br_bf = w_branch.astype(BF16)
    w_out_bf = w_out.astype(BF16)
    w1_bf = w_ff1.astype(BF16)
    w2_bf = w_ff2.astype(BF16)
    eye = jnp.eye(LRU_BLOCKS, dtype=F32)
    bd = lambda w: (eye[None, :, None, :, None] * w[:, :, :, None, :]).reshape(L, BRANCH_W, BRANCH_W)
    same_heads = np.array([0, 2, 5, 7])
    swap_heads = np.array([1, 3, 4, 6])
    sinks = jnp.stack([jnp.repeat(swa_sinks[:, same_heads], WINDOW, axis=1),
                       jnp.repeat(swa_sinks[:, swap_heads], WINDOW, axis=1)], axis=1)[..., None]
    stacked = dict(
        norm_mix=row(norm_mix), gm_v_gain=row(gm_v_gain), gm_ws=gm_ws,
        gm_bias=jnp.broadcast_to(jnp.swapaxes(gm_bs, 1, 2)[:, :, :, None],
                                 (L, GM_CHUNK, GM_GROUPS, LANES)).reshape(L, GM_CHUNK, BRANCH_W),
        conv_w=lru_conv_w, conv_b=row(lru_conv_b),
        w_ri=jnp.concatenate([bd(lru_wr), bd(lru_wi)], axis=2).astype(BF16),
        lru_br=row(lru_br), lru_bi=row(lru_bi), lam=row(lru_lambda),
        qk_gain=row(jnp.concatenate([jnp.tile(swa_q_gain, (1, SWA_HEADS)) * (HEAD_DIM ** -0.5),
                                     jnp.tile(swa_k_gain, (1, SWA_KV))], axis=1)),
        sinks=sinks,
        xa_q_gain=row(xa_q_gain), b_gate=b_gate,
    )
    shared = dict(ones_tile=jnp.asarray(
        np.kron(np.eye(2, dtype=np.float32), np.ones((HEAD_DIM, HEAD_DIM), np.float32)), BF16))
    rope = _rope_tables(S)
    norm_mlp_r = row(norm_mlp)

    kt, vx = _mem_kv(mem, norm_mem, w_mem_kv.astype(BF16), xa_k_gain)

    for l in range(L):
        x = _mixer(l, x, rope, kt, vx, w_in_bf, w_br_bf, w_out_bf, stacked, shared)
        x = _ffn(l, x.reshape(B * S, D), norm_mlp_r, w1_bf, w2_bf).reshape(B, S, D)
    return x
```

```python
import functools

import numpy as np
import jax
import jax.numpy as jnp
from jax import lax
from jax.experimental import pallas as pl
from jax.experimental.pallas import tpu as pltpu

D_MODEL = 1024
DEPTH = 4
N_MEM = 256
EPS = 1e-6
N_BRANCH = 4
BRANCH_W = 512
GM_GROUPS = 4
GM_CHUNK = 128
LRU_BLOCKS = 8
LRU_CONV = 4
LRU_C = 8.0
HEAD_DIM = 64
SWA_HEADS = 8
SWA_KV = 2
WINDOW = 128
ROPE_THETA = 500000.0
ROT_DIM = 16
XA_HEADS = 4
XA_DIM = 128
D_FF = 4 * D_MODEL

OFF_GM = 0
OFF_LRU = 1024
OFF_QKV = 2048
OFF_QX = 2816
OFF_GATE = 3328
D_IN = 7424

LANES = 128
SUBLANES = 8
T_MIX = 256
T_FFN = 512
VMEM_LIMIT = 56 * 1024 * 1024

F32 = jnp.float32
BF16 = jnp.bfloat16


def _dot(a, b):
    return jnp.dot(a, b, preferred_element_type=F32)


def _rms(x, gain_row, n):
    ms = jnp.sum(x * x, axis=-1, keepdims=True) * (1.0 / n)
    return x * lax.rsqrt(ms + EPS) * gain_row


GELU_C0 = 0.7978845608028654
GELU_C1 = GELU_C0 * 0.044715


def _gelu(x):
    hx = 0.5 * x
    return hx + hx * jnp.tanh(x * (GELU_C0 + GELU_C1 * (x * x)))


def _sigmoid(x):
    return 0.5 * jnp.tanh(0.5 * x) + 0.5


def _convert_rows(pairs):
    for src, dst in pairs:
        dst[...] = src[...].astype(BF16)


BF16_ROWS = 16


def _row_block_specs(shape, l, steps, flat_step):
    _, rows, cols = shape
    rb = rows // steps
    assert rb * steps == rows and rb % BF16_ROWS == 0, (shape, steps)
    return (pl.BlockSpec((None, rb, cols), lambda *idx: (l, flat_step(*idx), 0)),
            pl.BlockSpec((rb, cols), lambda *idx: (flat_step(*idx), 0)))


def _layer_spec(shape, l):
    nd = len(shape)
    return pl.BlockSpec((None,) + tuple(shape[1:]), lambda *_: (l,) + (0,) * (nd - 1),
                        pipeline_mode=pl.Buffered(1))


def _mem_kernel(mem_ref, nmem_ref, wkv_ref, kg_ref, kt_ref, vx_ref):
    mn = _rms(mem_ref[0], nmem_ref[0], D_MODEL).astype(BF16)
    kv = _dot(mn, wkv_ref[0])
    ones = jnp.ones((N_MEM, XA_DIM), BF16)
    for hd in range(XA_HEADS):
        kh = _rms(kv[:, hd * XA_DIM:(hd + 1) * XA_DIM], kg_ref[0], XA_DIM)
        kt_ref[0, 0, hd] = kh.T.astype(BF16)
        vh = kv[:, BRANCH_W + hd * XA_DIM:BRANCH_W + (hd + 1) * XA_DIM].astype(BF16)
        vx_ref[0, 0, hd] = jnp.concatenate([vh, ones], axis=1)


def _mem_kv(mem, norm_mem, w_mem_kv_bf, xa_k_gain):
    L, B = DEPTH, mem.shape[0]
    return pl.pallas_call(
        _mem_kernel,
        grid=(L, B),
        in_specs=[
            pl.BlockSpec((1, N_MEM, D_MODEL), lambda l, b: (b, 0, 0)),
            pl.BlockSpec((1, 1, D_MODEL), lambda l, b: (l, 0, 0)),
            pl.BlockSpec((1, D_MODEL, 2 * BRANCH_W), lambda l, b: (l, 0, 0)),
            pl.BlockSpec((1, 1, XA_DIM), lambda l, b: (l, 0, 0)),
        ],
        out_specs=[
            pl.BlockSpec((1, 1, XA_HEADS, XA_DIM, N_MEM), lambda l, b: (l, b, 0, 0, 0)),
            pl.BlockSpec((1, 1, XA_HEADS, N_MEM, 2 * XA_DIM), lambda l, b: (l, b, 0, 0, 0)),
        ],
        out_shape=[
            jax.ShapeDtypeStruct((L, B, XA_HEADS, XA_DIM, N_MEM), BF16),
            jax.ShapeDtypeStruct((L, B, XA_HEADS, N_MEM, 2 * XA_DIM), BF16),
        ],
        compiler_params=pltpu.CompilerParams(
            dimension_semantics=("arbitrary", "arbitrary")),
        name="mem_kv",
    )(mem, norm_mem.reshape(L, 1, D_MODEL), w_mem_kv_bf, xa_k_gain.reshape(L, 1, XA_DIM))


def _mixer_kernel(x_ref, cos_ref, sa_ref, sb_ref, kt_ref, vx_ref,
                  w_in_ref, w_br_ref, w_out_ref,
                  nmix_ref, gmv_ref, gmws_ref, gmb_ref,
                  cw_ref, cb_ref, wri_ref, br_ref, bi_ref, lam_ref,
                  qkg_ref, ones_ref, sink_ref, xag_ref, bg_ref, w1f_ref, w2f_ref,
                  o_ref, w1b_ref, w2b_ref,
                  kprev, vprev, xtail, hcar, xbuf, a_s, b_s, hl_s, ac_s, zs, *, T):
    t = pl.program_id(1)
    _convert_rows(((w1f_ref, w1b_ref), (w2f_ref, w2b_ref)))
    nw = T // WINDOW
    J = T // SUBLANES
    P = J + SUBLANES
    NS = BRANCH_W // LANES

    cur = lax.rem(t, 2)
    nxt = 1 - cur

    @pl.when(t == 0)
    def _():
        kprev[0] = jnp.zeros((WINDOW, 2 * LANES), BF16)
        vprev[0] = jnp.zeros((WINDOW, 3 * LANES), BF16)
        xtail[0] = jnp.zeros((SUBLANES, BRANCH_W), F32)
        hcar[0] = jnp.zeros((SUBLANES, BRANCH_W), F32)

    x = x_ref[0]
    hb = _rms(x, nmix_ref[...], D_MODEL).astype(BF16)

    def proj(lo, hi):
        return _dot(hb, w_in_ref[:, lo:hi])

    def staged(i, lo, hi):
        zs[i, :, 0:hi - lo] = proj(lo, hi)
        return zs[i, :, 0:hi - lo]

    zg = _gelu(staged(0, OFF_GM, OFF_GM + 2 * BRANCH_W))
    u = zg[:, :BRANCH_W]
    vn = _rms(zg[:, BRANCH_W:], gmv_ref[...], BRANCH_W).astype(BF16)
    r_i = lax.broadcasted_iota(jnp.int32, (GM_CHUNK, GM_CHUNK), 0)
    c_i = lax.broadcasted_iota(jnp.int32, (GM_CHUNK, GM_CHUNK), 1)
    causal = r_i >= c_i
    s_cols = []
    for g in range(GM_GROUPS):
        wg = jnp.where(causal, gmws_ref[g], 0.0).astype(BF16)
        vcat = jnp.concatenate(
            [vn[c * GM_CHUNK:(c + 1) * GM_CHUNK, g * LANES:(g + 1) * LANES] for c in range(nw)], axis=1)
        sg = _dot(wg, vcat)
        s_cols.append(jnp.concatenate(
            [sg[:, c * LANES:(c + 1) * LANES] for c in range(nw)], axis=0))
    s_full = jnp.concatenate(s_cols, axis=1) + jnp.concatenate([gmb_ref[...]] * nw, axis=0)
    o_gm = (u * s_full).astype(BF16)

    zl = staged(1, OFF_LRU, OFF_LRU + 2 * BRANCH_W)
    xb = zl[:, :BRANCH_W]
    gb = zl[:, BRANCH_W:]
    xbuf[0:SUBLANES, :] = xtail[cur]
    xbuf[SUBLANES:SUBLANES + T, :] = xb
    xtail[nxt] = xb[T - SUBLANES:, :]
    xc = cb_ref[...] + xb * cw_ref[3:4, :]
    for k in range(LRU_CONV - 1):
        sh = LRU_CONV - 1 - k
        xc = xc + xbuf[SUBLANES - sh:SUBLANES - sh + T, :] * cw_ref[k:k + 1, :]
    ri = _dot(xc.astype(BF16), wri_ref[...])
    r_gate = _sigmoid(ri[:, :BRANCH_W] + br_ref[...])
    i_gate = _sigmoid(ri[:, BRANCH_W:] + bi_ref[...])
    nl = -lam_ref[...]
    softplus = jnp.maximum(nl, 0.0) + jnp.log1p(jnp.exp(-jnp.abs(nl)))
    log_a = (-LRU_C) * r_gate * softplus
    a = jnp.exp(log_a)
    b_in = xc * i_gate * jnp.sqrt(1.0 - a * a)
    for s in range(SUBLANES):
        for sl in range(NS):
            a_s[sl, s * P:s * P + J, :] = a[s * J:(s + 1) * J, sl * LANES:(sl + 1) * LANES]
            b_s[sl, s * P:s * P + J, :] = b_in[s * J:(s + 1) * J, sl * LANES:(sl + 1) * LANES]

    def scan_body(j, carry):
        new = []
        for sl in range(NS):
            hl, ac = carry[2 * sl], carry[2 * sl + 1]
            aj = a_s[sl, pl.ds(j, SUBLANES, stride=P), :]
            bj = b_s[sl, pl.ds(j, SUBLANES, stride=P), :]
            hl = aj * hl + bj
            ac = aj * ac
            hl_s[sl, pl.ds(j, SUBLANES, stride=P), :] = hl
            ac_s[sl, pl.ds(j, SUBLANES, stride=P), :] = ac
            new += [hl, ac]
        return tuple(new)

    init = tuple(jnp.zeros((SUBLANES, LANES), F32) if i % 2 == 0 else jnp.ones((SUBLANES, LANES), F32)
                 for i in range(2 * NS))
    fin = lax.fori_loop(0, J, scan_body, init)
    h_cols = []
    for sl in range(NS):
        hl_end, ac_end = fin[2 * sl], fin[2 * sl + 1]
        hin = hcar[cur, 0:1, sl * LANES:(sl + 1) * LANES]
        segs = []
        for s in range(SUBLANES):
            segs.append(hl_s[sl, s * P:s * P + J, :] + ac_s[sl, s * P:s * P + J, :] * hin)
            hin = hl_end[s:s + 1, :] + ac_end[s:s + 1, :] * hin
        hcar[nxt, 0:1, sl * LANES:(sl + 1) * LANES] = hin
        h_cols.append(jnp.concatenate(segs, axis=0))
    h_lru = jnp.concatenate(h_cols, axis=1)
    o_lru = (_gelu(gb) * h_lru).astype(BF16)

    zq = proj(OFF_QKV, OFF_QKV + 768)
    vv = zq[:, 640:768]
    cos_t, sa_t, sb_t = cos_ref[...], sa_ref[...], sb_ref[...]
    roped = []
    for p in range(5):
        xt = zq[:, p * LANES:(p + 1) * LANES]
        ssq = _dot((xt * xt).astype(BF16), ones_ref[...])
        xt = xt * lax.rsqrt(ssq * (1.0 / HEAD_DIM) + EPS) * qkg_ref[:, p * LANES:(p + 1) * LANES]
        roped.append(xt * cos_t + pltpu.roll(xt, LANES - ROT_DIM // 2, 1) * sa_t
                     + pltpu.roll(xt, ROT_DIM // 2, 1) * sb_t)
    low_half = lax.broadcasted_iota(jnp.int32, (T, LANES), 1) < HEAD_DIM
    q_lo = [jnp.where(low_half, roped[p], 0.0).astype(BF16) for p in range(4)]
    q_hi = [jnp.where(low_half, 0.0, roped[p]).astype(BF16) for p in range(4)]
    half = SWA_HEADS // SWA_KV // 2
    q_same = q_lo[:half] + q_hi[half:]
    q_swap = q_hi[:half] + q_lo[half:]
    ones_t = jnp.ones((T, LANES), BF16)
    k2 = jnp.concatenate([roped[4].astype(BF16),
                          pltpu.roll(roped[4], HEAD_DIM, 1).astype(BF16)], axis=1)
    v3 = jnp.concatenate([vv.astype(BF16), ones_t,
                          pltpu.roll(vv, HEAD_DIM, 1).astype(BF16)], axis=1)
    kext = jnp.concatenate([kprev[cur], k2], axis=0)
    vext = jnp.concatenate([vprev[cur], v3], axis=0)
    kprev[nxt] = k2[T - WINDOW:, :]
    vprev[nxt] = v3[T - WINDOW:, :]
    qi = lax.broadcasted_iota(jnp.int32, (WINDOW, 2 * WINDOW), 0)
    kj = lax.broadcasted_iota(jnp.int32, (WINDOW, 2 * WINDOW), 1)
    diff = qi + WINDOW - kj
    band = jnp.where(diff >= 0, diff, WINDOW) < WINDOW
    bias = jnp.where(band, 0.0, -jnp.inf)
    kj0 = kj + jnp.where(t > 0, WINDOW, 0)
    bias0 = jnp.where(kj0 >= WINDOW, bias, -jnp.inf)
    lane_w = lax.broadcasted_iota(jnp.int32, (WINDOW, LANES), 1) < HEAD_DIM
    nslot = SWA_HEADS // 2

    def attend(qs, kw, vw, sink, bias_w):
        s = lax.dot_general(qs, kw, (((1,), (1,)), ((), ())), preferred_element_type=F32)
        s = s + jnp.concatenate([bias_w] * nslot, axis=0)
        m = jnp.maximum(jnp.max(s, axis=-1, keepdims=True), sink)
        pr = jnp.exp(s - m).astype(BF16)
        oe = _dot(pr, vw)
        return oe[:, :LANES] / (oe[:, LANES:] + jnp.exp(sink - m))

    o_win = []
    for n in range(nw):
        rows = slice(n * WINDOW, (n + 1) * WINDOW)
        krows = slice(n * WINDOW, (n + 2) * WINDOW)
        bias_w = bias0 if n == 0 else bias
        o_same = attend(jnp.concatenate([q[rows] for q in q_same], axis=0),
                        kext[krows, :LANES], vext[krows, :2 * LANES], sink_ref[0], bias_w)
        o_swap = attend(jnp.concatenate([q[rows] for q in q_swap], axis=0),
                        kext[krows, LANES:],
                        jnp.concatenate([vext[krows, 2 * LANES:], vext[krows, LANES:2 * LANES]], axis=1),
                        sink_ref[1], bias_w)
        tiles = []
        for p in range(4):
            blk = slice(p * WINDOW, (p + 1) * WINDOW)
            lo, hi = (o_same[blk], o_swap[blk]) if p < half else (o_swap[blk], o_same[blk])
            tiles.append(jnp.where(lane_w, lo, hi))
        o_win.append(jnp.concatenate(tiles, axis=1))
    o_swa = jnp.concatenate(o_win, axis=0).astype(BF16)

    zx = proj(OFF_QX, OFF_QX + BRANCH_W)
    xg = xag_ref[...] * (XA_DIM ** -0.5)
    o_heads = []
    for hd in range(XA_HEADS):
        qn = _rms(zx[:, hd * XA_DIM:(hd + 1) * XA_DIM], xg, XA_DIM).astype(BF16)
        s = _dot(qn, kt_ref[hd])
        pr = jnp.exp(s - jnp.max(s, axis=-1, keepdims=True)).astype(BF16)
        oe = _dot(pr, vx_ref[hd])
        o_heads.append(oe[:, :XA_DIM] / oe[:, XA_DIM:])
    o_xa = jnp.concatenate(o_heads, axis=1).astype(BF16)

    acc = None
    for b, ob in enumerate((o_gm, o_lru, o_swa, o_xa)):
        gate = _sigmoid(staged(2 + b, OFF_GATE + b * D_MODEL, OFF_GATE + (b + 1) * D_MODEL) + bg_ref[b:b + 1, :])
        contrib = gate * _dot(ob, w_br_ref[b])
        acc = contrib if acc is None else acc + contrib
    o_ref[0] = x + _dot(acc.astype(BF16), w_out_ref[...])


def _mixer(l, x, rope, kt, vx, w_in, w_br, w_out, stacked, shared, w_ff1, w_ff2):
    B, S, D = x.shape
    T = T_MIX
    nt = S // T
    J = T // SUBLANES
    P = J + SUBLANES
    NS = BRANCH_W // LANES
    vmem = pl.BlockSpec(memory_space=pltpu.VMEM)
    tok = pl.BlockSpec((1, T, D), lambda b, t: (b, t, 0))
    rp = pl.BlockSpec((T, LANES), lambda b, t: (t, 0))
    flat = lambda b, t: b * nt + t
    w1_in, w1_out = _row_block_specs(w_ff1.shape, l, B * nt, flat)
    w2_in, w2_out = _row_block_specs(w_ff2.shape, l, B * nt, flat)
    s = stacked
    operands = [
        (x, tok), (rope[0], rp), (rope[1], rp), (rope[2], rp),
        (kt, pl.BlockSpec((None, None, XA_HEADS, XA_DIM, N_MEM), lambda b, t: (l, b, 0, 0, 0))),
        (vx, pl.BlockSpec((None, None, XA_HEADS, N_MEM, 2 * XA_DIM), lambda b, t: (l, b, 0, 0, 0))),
        (w_in, vmem), (w_br, vmem), (w_out, vmem),
    ]
    for arr in (s["norm_mix"], s["gm_v_gain"], s["gm_ws"], s["gm_bias"],
                s["conv_w"], s["conv_b"], s["w_ri"], s["lru_br"], s["lru_bi"], s["lam"], s["qk_gain"]):
        operands.append((arr, _layer_spec(arr.shape, l)))
    operands.append((shared["ones_tile"], vmem))
    for arr in (s["sinks"], s["xa_q_gain"], s["b_gate"]):
        operands.append((arr, _layer_spec(arr.shape, l)))
    operands += [(w_ff1, w1_in), (w_ff2, w2_in)]
    return pl.pallas_call(
        functools.partial(_mixer_kernel, T=T),
        grid=(B, nt),
        in_specs=[spec for _, spec in operands],
        out_specs=[tok, w1_out, w2_out],
        out_shape=[jax.ShapeDtypeStruct((B, S, D), F32),
                   jax.ShapeDtypeStruct(w_ff1.shape[1:], BF16),
                   jax.ShapeDtypeStruct(w_ff2.shape[1:], BF16)],
        scratch_shapes=[
            pltpu.VMEM((2, WINDOW, 2 * LANES), BF16),
            pltpu.VMEM((2, WINDOW, 3 * LANES), BF16),
            pltpu.VMEM((2, SUBLANES, BRANCH_W), F32),
            pltpu.VMEM((2, SUBLANES, BRANCH_W), F32),
            pltpu.VMEM((T + SUBLANES, BRANCH_W), F32),
            pltpu.VMEM((NS, SUBLANES * P, LANES), F32),
            pltpu.VMEM((NS, SUBLANES * P, LANES), F32),
            pltpu.VMEM((NS, SUBLANES * P, LANES), F32),
            pltpu.VMEM((NS, SUBLANES * P, LANES), F32),
            pltpu.VMEM((2 + N_BRANCH, T, D_MODEL), F32),
        ],
        compiler_params=pltpu.CompilerParams(
            dimension_semantics=("arbitrary", "arbitrary"),
            vmem_limit_bytes=VMEM_LIMIT),
        name="mixer",
    )(*[arr for arr, _ in operands])


def _ffn_kernel(x_ref, g_ref, w1_ref, w2_ref, *rest):
    n_next = (len(rest) - 1) // 2
    o_ref = rest[n_next]
    _convert_rows(zip(rest[:n_next], rest[n_next + 1:]))
    x = x_ref[...]
    hb = _rms(x, g_ref[...], D_MODEL).astype(BF16)
    acc = x
    for c in range(D_FF // D_MODEL):
        hid = _dot(hb, w1_ref[:, c * D_MODEL:(c + 1) * D_MODEL])
        hid = jnp.square(jnp.maximum(hid, 0.0)).astype(BF16)
        acc = acc + _dot(hid, w2_ref[c * D_MODEL:(c + 1) * D_MODEL, :])
    o_ref[...] = acc


def _ffn(l, x2, g, w1, w2, next_f32):
    N, D = x2.shape
    steps = N // T_FFN
    vmem = pl.BlockSpec(memory_space=pltpu.VMEM)
    tok = pl.BlockSpec((T_FFN, D), lambda i: (i, 0))
    conv = [_row_block_specs(a.shape, l + 1, steps, lambda i: i) for a in next_f32]
    outs = pl.pallas_call(
        _ffn_kernel,
        grid=(steps,),
        in_specs=[tok, _layer_spec(g.shape, l), vmem, vmem] + [c[0] for c in conv],
        out_specs=[tok] + [c[1] for c in conv],
        out_shape=[jax.ShapeDtypeStruct((N, D), F32)]
        + [jax.ShapeDtypeStruct(a.shape[1:], BF16) for a in next_f32],
        compiler_params=pltpu.CompilerParams(
            dimension_semantics=("arbitrary",), vmem_limit_bytes=VMEM_LIMIT),
        name="ffn",
    )(x2, g, w1, w2, *next_f32)
    return outs[0], outs[1:]


def _rope_tables(seq):
    pos = jnp.arange(seq, dtype=F32)
    half = ROT_DIM // 2
    inv = ROPE_THETA ** (-jnp.arange(0, ROT_DIM, 2, dtype=F32) / ROT_DIM)
    ang = pos[:, None] * inv[None, :]
    cos, sin = jnp.cos(ang), jnp.sin(ang)
    d = np.arange(LANES) % HEAD_DIM
    f = np.arange(half)[:, None]
    e_cos = ((d[None, :] < ROT_DIM) & (d[None, :] % half == f)).astype(np.float32)
    e_a = -((d[None, :] < half) & (d[None, :] == f)).astype(np.float32)
    e_b = ((d[None, :] >= half) & (d[None, :] < ROT_DIM) & (d[None, :] - half == f)).astype(np.float32)
    expand = lambda tbl, e: jnp.dot(tbl, jnp.asarray(e), precision=lax.Precision.HIGHEST)
    return (expand(cos, e_cos) + jnp.asarray((d >= ROT_DIM).astype(np.float32))[None, :],
            expand(sin, e_a), expand(sin, e_b))


@jax.jit
def kernel(x, mem, norm_mix, norm_mem, norm_mlp, w_in, b_gate, gm_v_gain, gm_ws, gm_bs, lru_conv_w, lru_conv_b, lru_wr, lru_br, lru_wi, lru_bi, lru_lambda, swa_q_gain, swa_k_gain, swa_sinks, w_mem_kv, xa_q_gain, xa_k_gain, w_branch, w_out, w_ff1, w_ff2):
    B, S, D = x.shape
    L = DEPTH
    row = lambda a: a[:, None, :]

    eye = jnp.eye(LRU_BLOCKS, dtype=F32)
    bd = lambda w: (eye[None, :, None, :, None] * w[:, :, :, None, :]).reshape(L, BRANCH_W, BRANCH_W)
    same_heads = np.array([0, 2, 5, 7])
    swap_heads = np.array([1, 3, 4, 6])
    sinks = jnp.stack([jnp.repeat(swa_sinks[:, same_heads], WINDOW, axis=1),
                       jnp.repeat(swa_sinks[:, swap_heads], WINDOW, axis=1)], axis=1)[..., None]
    stacked = dict(
        norm_mix=row(norm_mix), gm_v_gain=row(gm_v_gain), gm_ws=gm_ws,
        gm_bias=jnp.broadcast_to(jnp.swapaxes(gm_bs, 1, 2)[:, :, :, None],
                                 (L, GM_CHUNK, GM_GROUPS, LANES)).reshape(L, GM_CHUNK, BRANCH_W),
        conv_w=lru_conv_w, conv_b=row(lru_conv_b),
        w_ri=jnp.concatenate([bd(lru_wr), bd(lru_wi)], axis=2).astype(BF16),
        lru_br=row(lru_br), lru_bi=row(lru_bi), lam=row(lru_lambda),
        qk_gain=row(jnp.concatenate([jnp.tile(swa_q_gain, (1, SWA_HEADS)) * (HEAD_DIM ** -0.5),
                                     jnp.tile(swa_k_gain, (1, SWA_KV))], axis=1)),
        sinks=sinks,
        xa_q_gain=row(xa_q_gain), b_gate=b_gate,
    )
    shared = dict(ones_tile=jnp.asarray(
        np.kron(np.eye(2, dtype=np.float32), np.ones((HEAD_DIM, HEAD_DIM), np.float32)), BF16))
    rope = _rope_tables(S)
    norm_mlp_r = row(norm_mlp)
    w_br2d = w_branch.reshape(L, N_BRANCH * BRANCH_W, D_MODEL)

    kt, vx = _mem_kv(mem, norm_mem, w_mem_kv.astype(BF16), xa_k_gain)

    mix_w = (w_in[0].astype(BF16), w_br2d[0].astype(BF16), w_out[0].astype(BF16))
    for l in range(L):
        w_in_l, w_br_l, w_out_l = mix_w
        x, w1_l, w2_l = _mixer(l, x, rope, kt, vx, w_in_l, w_br_l.reshape(N_BRANCH, BRANCH_W, D_MODEL),
                               w_out_l, stacked, shared, w_ff1, w_ff2)
        nxt = (w_in, w_br2d, w_out) if l + 1 < L else ()
        x2, mix_w = _ffn(l, x.reshape(B * S, D), norm_mlp_r, w1_l, w2_l, nxt)
        x = x2.reshape(B, S, D)
    return x
```

```python
import functools

import numpy as np
import jax
import jax.numpy as jnp
from jax import lax
from jax.experimental import pallas as pl
from jax.experimental.pallas import tpu as pltpu

D_MODEL = 1024
DEPTH = 4
N_MEM = 256
EPS = 1e-6
N_BRANCH = 4
BRANCH_W = 512
GM_GROUPS = 4
GM_CHUNK = 128
LRU_BLOCKS = 8
LRU_CONV = 4
LRU_C = 8.0
HEAD_DIM = 64
SWA_HEADS = 8
SWA_KV = 2
WINDOW = 128
ROPE_THETA = 500000.0
ROT_DIM = 16
XA_HEADS = 4
XA_DIM = 128
D_FF = 4 * D_MODEL

OFF_GM = 0
OFF_LRU = 1024
OFF_QKV = 2048
OFF_QX = 2816
OFF_GATE = 3328
D_IN = 7424

LANES = 128
SUBLANES = 8
T_MIX = 256
T_FFN = 512
MIX_LAG = 2
VMEM_LIMIT = 56 * 1024 * 1024

F32 = jnp.float32
BF16 = jnp.bfloat16


def _dot(a, b):
    return jnp.dot(a, b, preferred_element_type=F32)


def _rms(x, gain_row, n):
    ms = jnp.sum(x * x, axis=-1, keepdims=True) * (1.0 / n)
    return x * lax.rsqrt(ms + EPS) * gain_row


GELU_C0 = 0.7978845608028654
GELU_C1 = GELU_C0 * 0.044715


def _gelu(x):
    hx = 0.5 * x
    return hx + hx * jnp.tanh(x * (GELU_C0 + GELU_C1 * (x * x)))


def _sigmoid(x):
    return 0.5 * jnp.tanh(0.5 * x) + 0.5


def _convert_rows(pairs):
    for src, dst in pairs:
        dst[...] = src[...].astype(BF16)


BF16_ROWS = 16


def _row_block_specs(shape, l, steps, flat_step):
    _, rows, cols = shape
    rb = rows // steps
    assert rb * steps == rows and rb % BF16_ROWS == 0, (shape, steps)
    return (pl.BlockSpec((None, rb, cols), lambda *idx: (l, flat_step(*idx), 0)),
            pl.BlockSpec((rb, cols), lambda *idx: (flat_step(*idx), 0)))


def _layer_spec(shape, l):
    nd = len(shape)
    return pl.BlockSpec((None,) + tuple(shape[1:]), lambda *_: (l,) + (0,) * (nd - 1),
                        pipeline_mode=pl.Buffered(1))


def _mem_kernel(mem_ref, nmem_ref, wkv_ref, kg_ref, kt_ref, vx_ref):
    mn = _rms(mem_ref[0], nmem_ref[0], D_MODEL).astype(BF16)
    kv = _dot(mn, wkv_ref[0])
    ones = jnp.ones((N_MEM, XA_DIM), BF16)
    for hd in range(XA_HEADS):
        kh = _rms(kv[:, hd * XA_DIM:(hd + 1) * XA_DIM], kg_ref[0], XA_DIM)
        kt_ref[0, 0, hd] = kh.T.astype(BF16)
        vh = kv[:, BRANCH_W + hd * XA_DIM:BRANCH_W + (hd + 1) * XA_DIM].astype(BF16)
        vx_ref[0, 0, hd] = jnp.concatenate([vh, ones], axis=1)


def _mem_kv(mem, norm_mem, w_mem_kv_bf, xa_k_gain):
    L, B = DEPTH, mem.shape[0]
    return pl.pallas_call(
        _mem_kernel,
        grid=(L, B),
        in_specs=[
            pl.BlockSpec((1, N_MEM, D_MODEL), lambda l, b: (b, 0, 0)),
            pl.BlockSpec((1, 1, D_MODEL), lambda l, b: (l, 0, 0)),
            pl.BlockSpec((1, D_MODEL, 2 * BRANCH_W), lambda l, b: (l, 0, 0)),
            pl.BlockSpec((1, 1, XA_DIM), lambda l, b: (l, 0, 0)),
        ],
        out_specs=[
            pl.BlockSpec((1, 1, XA_HEADS, XA_DIM, N_MEM), lambda l, b: (l, b, 0, 0, 0)),
            pl.BlockSpec((1, 1, XA_HEADS, N_MEM, 2 * XA_DIM), lambda l, b: (l, b, 0, 0, 0)),
        ],
        out_shape=[
            jax.ShapeDtypeStruct((L, B, XA_HEADS, XA_DIM, N_MEM), BF16),
            jax.ShapeDtypeStruct((L, B, XA_HEADS, N_MEM, 2 * XA_DIM), BF16),
        ],
        compiler_params=pltpu.CompilerParams(
            dimension_semantics=("arbitrary", "arbitrary")),
        name="mem_kv",
    )(mem, norm_mem.reshape(L, 1, D_MODEL), w_mem_kv_bf, xa_k_gain.reshape(L, 1, XA_DIM))


def _mixer_kernel(x_ref, cos_ref, sa_ref, sb_ref, kt_ref, vx_ref,
                  w_in_ref, w_br_ref, w_out_ref,
                  nmix_ref, gmv_ref, gmws_ref, gmb_ref,
                  cw_ref, cb_ref, wri_ref, br_ref, bi_ref, lam_ref,
                  qkg_ref, ones_ref, sink_ref, xag_ref, bg_ref, w1f_ref, w2f_ref,
                  o_ref, w1b_ref, w2b_ref,
                  kprev, vprev, xtail, hcar, xbuf, a_s, b_s, hl_s, ac_s, *, T, NB):
    t = pl.program_id(0)
    _convert_rows(((w1f_ref, w1b_ref), (w2f_ref, w2b_ref)))
    nw = T // WINDOW
    J = T // SUBLANES
    P = J + SUBLANES
    NS = BRANCH_W // LANES

    cur = lax.rem(t, 2)
    nxt = 1 - cur

    @pl.when(t == 0)
    def _():
        for bi in range(NB):
            kprev[bi, 0] = jnp.zeros((WINDOW, 2 * LANES), BF16)
            vprev[bi, 0] = jnp.zeros((WINDOW, 3 * LANES), BF16)
            xtail[bi, 0] = jnp.zeros((SUBLANES, BRANCH_W), F32)
            hcar[bi, 0] = jnp.zeros((SUBLANES, BRANCH_W), F32)

    r_i = lax.broadcasted_iota(jnp.int32, (GM_CHUNK, GM_CHUNK), 0)
    c_i = lax.broadcasted_iota(jnp.int32, (GM_CHUNK, GM_CHUNK), 1)
    causal = r_i >= c_i
    qi = lax.broadcasted_iota(jnp.int32, (WINDOW, 2 * WINDOW), 0)
    kj = lax.broadcasted_iota(jnp.int32, (WINDOW, 2 * WINDOW), 1)
    diff = qi + WINDOW - kj
    band = jnp.where(diff >= 0, diff, WINDOW) < WINDOW
    bias = jnp.where(band, 0.0, -jnp.inf)
    kj0 = kj + jnp.where(t > 0, WINDOW, 0)
    bias0 = jnp.where(kj0 >= WINDOW, bias, -jnp.inf)
    lane_w = lax.broadcasted_iota(jnp.int32, (WINDOW, LANES), 1) < HEAD_DIM
    low_half = lax.broadcasted_iota(jnp.int32, (T, LANES), 1) < HEAD_DIM
    nslot = SWA_HEADS // 2
    half = SWA_HEADS // SWA_KV // 2
    scan_out = {}

    def attend(qs, kw, vw, sink, bias_w):
        s = lax.dot_general(qs, kw, (((1,), (1,)), ((), ())), preferred_element_type=F32)
        s = s + jnp.concatenate([bias_w] * nslot, axis=0)
        m = jnp.maximum(jnp.max(s, axis=-1, keepdims=True), sink)
        pr = jnp.exp(s - m).astype(BF16)
        oe = _dot(pr, vw)
        return oe[:, :LANES] / (oe[:, LANES:] + jnp.exp(sink - m))

    def stream(bi):
        x = x_ref[bi]
        hb = _rms(x, nmix_ref[...], D_MODEL).astype(BF16)

        def proj(lo, hi):
            return _dot(hb, w_in_ref[:, lo:hi])

        def merge(b, ob, acc):
            gate = _sigmoid(
                proj(OFF_GATE + b * D_MODEL, OFF_GATE + (b + 1) * D_MODEL) + bg_ref[b:b + 1, :])
            contrib = gate * _dot(ob, w_br_ref[b])
            return contrib if acc is None else acc + contrib
        yield

        zg = _gelu(proj(OFF_GM, OFF_GM + 2 * BRANCH_W))
        yield
        u = zg[:, :BRANCH_W]
        vn = _rms(zg[:, BRANCH_W:], gmv_ref[...], BRANCH_W).astype(BF16)
        s_cols = []
        for g in range(GM_GROUPS):
            wg = jnp.where(causal, gmws_ref[g], 0.0).astype(BF16)
            vcat = jnp.concatenate(
                [vn[c * GM_CHUNK:(c + 1) * GM_CHUNK, g * LANES:(g + 1) * LANES] for c in range(nw)], axis=1)
            sg = _dot(wg, vcat)
            s_cols.append(jnp.concatenate(
                [sg[:, c * LANES:(c + 1) * LANES] for c in range(nw)], axis=0))
        s_full = jnp.concatenate(s_cols, axis=1) + jnp.concatenate([gmb_ref[...]] * nw, axis=0)
        o_gm = (u * s_full).astype(BF16)
        yield

        zl = proj(OFF_LRU, OFF_LRU + 2 * BRANCH_W)
        xb = zl[:, :BRANCH_W]
        gb = zl[:, BRANCH_W:]
        xbuf[bi, 0:SUBLANES, :] = xtail[bi, cur]
        xbuf[bi, SUBLANES:SUBLANES + T, :] = xb
        xtail[bi, nxt] = xb[T - SUBLANES:, :]
        yield
        xc = cb_ref[...] + xb * cw_ref[3:4, :]
        for k in range(LRU_CONV - 1):
            sh = LRU_CONV - 1 - k
            xc = xc + xbuf[bi, SUBLANES - sh:SUBLANES - sh + T, :] * cw_ref[k:k + 1, :]
        ri = _dot(xc.astype(BF16), wri_ref[...])
        yield
        r_gate = _sigmoid(ri[:, :BRANCH_W] + br_ref[...])
        i_gate = _sigmoid(ri[:, BRANCH_W:] + bi_ref[...])
        nl = -lam_ref[...]
        softplus = jnp.maximum(nl, 0.0) + jnp.log1p(jnp.exp(-jnp.abs(nl)))
        log_a = (-LRU_C) * r_gate * softplus
        a = jnp.exp(log_a)
        b_in = xc * i_gate * jnp.sqrt(1.0 - a * a)
        for s in range(SUBLANES):
            for sl in range(NS):
                a_s[bi * NS + sl, s * P:s * P + J, :] = a[s * J:(s + 1) * J, sl * LANES:(sl + 1) * LANES]
                b_s[bi * NS + sl, s * P:s * P + J, :] = b_in[s * J:(s + 1) * J, sl * LANES:(sl + 1) * LANES]
        acc = merge(0, o_gm, None)
        yield "scan"
        fin = scan_out["fin"]

        h_cols = []
        for sl in range(NS):
            k = bi * NS + sl
            hl_end, ac_end = fin[2 * k], fin[2 * k + 1]
            hin = hcar[bi, cur, 0:1, sl * LANES:(sl + 1) * LANES]
            segs = []
            for s in range(SUBLANES):
                segs.append(hl_s[k, s * P:s * P + J, :] + ac_s[k, s * P:s * P + J, :] * hin)
                hin = hl_end[s:s + 1, :] + ac_end[s:s + 1, :] * hin
            hcar[bi, nxt, 0:1, sl * LANES:(sl + 1) * LANES] = hin
            h_cols.append(jnp.concatenate(segs, axis=0))
        h_lru = jnp.concatenate(h_cols, axis=1)
        o_lru = (_gelu(gb) * h_lru).astype(BF16)
        yield
        acc = merge(1, o_lru, acc)
        yield

        zq = proj(OFF_QKV, OFF_QKV + 768)
        vv = zq[:, 640:768]
        cos_t, sa_t, sb_t = cos_ref[...], sa_ref[...], sb_ref[...]
        roped = []
        for p in range(5):
            xt = zq[:, p * LANES:(p + 1) * LANES]
            ssq = _dot((xt * xt).astype(BF16), ones_ref[...])
            xt = xt * lax.rsqrt(ssq * (1.0 / HEAD_DIM) + EPS) * qkg_ref[:, p * LANES:(p + 1) * LANES]
            roped.append(xt * cos_t + pltpu.roll(xt, LANES - ROT_DIM // 2, 1) * sa_t
                         + pltpu.roll(xt, ROT_DIM // 2, 1) * sb_t)
        yield
        q_lo = [jnp.where(low_half, roped[p], 0.0).astype(BF16) for p in range(4)]
        q_hi = [jnp.where(low_half, 0.0, roped[p]).astype(BF16) for p in range(4)]
        q_same = q_lo[:half] + q_hi[half:]
        q_swap = q_hi[:half] + q_lo[half:]
        ones_t = jnp.ones((T, LANES), BF16)
        k2 = jnp.concatenate([roped[4].astype(BF16),
                              pltpu.roll(roped[4], HEAD_DIM, 1).astype(BF16)], axis=1)
        v3 = jnp.concatenate([vv.astype(BF16), ones_t,
                              pltpu.roll(vv, HEAD_DIM, 1).astype(BF16)], axis=1)
        kext = jnp.concatenate([kprev[bi, cur], k2], axis=0)
        vext = jnp.concatenate([vprev[bi, cur], v3], axis=0)
        kprev[bi, nxt] = k2[T - WINDOW:, :]
        vprev[bi, nxt] = v3[T - WINDOW:, :]
        o_win = []
        for n in range(nw):
            rows = slice(n * WINDOW, (n + 1) * WINDOW)
            krows = slice(n * WINDOW, (n + 2) * WINDOW)
            bias_w = bias0 if n == 0 else bias
            o_same = attend(jnp.concatenate([q[rows] for q in q_same], axis=0),
                            kext[krows, :LANES], vext[krows, :2 * LANES], sink_ref[0], bias_w)
            yield
            o_swap = attend(jnp.concatenate([q[rows] for q in q_swap], axis=0),
                            kext[krows, LANES:],
                            jnp.concatenate([vext[krows, 2 * LANES:], vext[krows, LANES:2 * LANES]], axis=1),
                            sink_ref[1], bias_w)
            tiles = []
            for p in range(4):
                blk = slice(p * WINDOW, (p + 1) * WINDOW)
                lo, hi = (o_same[blk], o_swap[blk]) if p < half else (o_swap[blk], o_same[blk])
                tiles.append(jnp.where(lane_w, lo, hi))
            o_win.append(jnp.concatenate(tiles, axis=1))
            yield
        o_swa = jnp.concatenate(o_win, axis=0).astype(BF16)
        acc = merge(2, o_swa, acc)
        yield

        zx = proj(OFF_QX, OFF_QX + BRANCH_W)
        xg = xag_ref[...] * (XA_DIM ** -0.5)
        o_heads = []
        for hd in range(XA_HEADS):
            qn = _rms(zx[:, hd * XA_DIM:(hd + 1) * XA_DIM], xg, XA_DIM).astype(BF16)
            s = _dot(qn, kt_ref[bi, hd])
            pr = jnp.exp(s - jnp.max(s, axis=-1, keepdims=True)).astype(BF16)
            oe = _dot(pr, vx_ref[bi, hd])
            o_heads.append(oe[:, :XA_DIM] / oe[:, XA_DIM:])
            yield
        o_xa = jnp.concatenate(o_heads, axis=1).astype(BF16)

        acc = merge(3, o_xa, acc)
        yield
        o_ref[bi] = x + _dot(acc.astype(BF16), w_out_ref[...])

    def scan_body(j, carry):
        new = []
        for k in range(NB * NS):
            hl, ac = carry[2 * k], carry[2 * k + 1]
            aj = a_s[k, pl.ds(j, SUBLANES, stride=P), :]
            bj = b_s[k, pl.ds(j, SUBLANES, stride=P), :]
            hl = aj * hl + bj
            ac = aj * ac
            hl_s[k, pl.ds(j, SUBLANES, stride=P), :] = hl
            ac_s[k, pl.ds(j, SUBLANES, stride=P), :] = ac
            new += [hl, ac]
        return tuple(new)

    gens = [stream(bi) for bi in range(NB)]
    done = [False] * NB
    at_scan = [False] * NB
    steps = [0] * NB

    def advance(i):
        try:
            if next(gens[i]) == "scan":
                at_scan[i] = True
            steps[i] += 1
        except StopIteration:
            done[i] = True

    def run_until(pred):
        while not pred():
            for i in range(NB):
                lead_ok = i == 0 or done[i - 1] or at_scan[i - 1] or steps[i - 1] - steps[i] >= MIX_LAG
                if not done[i] and not at_scan[i] and lead_ok:
                    advance(i)

    run_until(lambda: all(at_scan))
    init = tuple(jnp.zeros((SUBLANES, LANES), F32) if i % 2 == 0 else jnp.ones((SUBLANES, LANES), F32)
                 for i in range(2 * NB * NS))
    scan_out["fin"] = lax.fori_loop(0, J, scan_body, init)
    at_scan[:] = [False] * NB
    steps[:] = [0] * NB
    run_until(lambda: all(done))


def _mixer(l, x, rope, kt, vx, w_in, w_br, w_out, stacked, shared, w_ff1, w_ff2):
    B, S, D = x.shape
    T = T_MIX
    nt = S // T
    J = T // SUBLANES
    P = J + SUBLANES
    NS = BRANCH_W // LANES
    vmem = pl.BlockSpec(memory_space=pltpu.VMEM)
    tok = pl.BlockSpec((B, T, D), lambda t: (0, t, 0))
    rp = pl.BlockSpec((T, LANES), lambda t: (t, 0))
    w1_in, w1_out = _row_block_specs(w_ff1.shape, l, nt, lambda t: t)
    w2_in, w2_out = _row_block_specs(w_ff2.shape, l, nt, lambda t: t)
    s = stacked
    operands = [
        (x, tok), (rope[0], rp), (rope[1], rp), (rope[2], rp),
        (kt, pl.BlockSpec((None, B, XA_HEADS, XA_DIM, N_MEM), lambda t: (l, 0, 0, 0, 0))),
        (vx, pl.BlockSpec((None, B, XA_HEADS, N_MEM, 2 * XA_DIM), lambda t: (l, 0, 0, 0, 0))),
        (w_in, vmem), (w_br, vmem), (w_out, vmem),
    ]
    for arr in (s["norm_mix"], s["gm_v_gain"], s["gm_ws"], s["gm_bias"],
                s["conv_w"], s["conv_b"], s["w_ri"], s["lru_br"], s["lru_bi"], s["lam"], s["qk_gain"]):
        operands.append((arr, _layer_spec(arr.shape, l)))
    operands.append((shared["ones_tile"], vmem))
    for arr in (s["sinks"], s["xa_q_gain"], s["b_gate"]):
        operands.append((arr, _layer_spec(arr.shape, l)))
    operands += [(w_ff1, w1_in), (w_ff2, w2_in)]
    return pl.pallas_call(
        functools.partial(_mixer_kernel, T=T, NB=B),
        grid=(nt,),
        in_specs=[spec for _, spec in operands],
        out_specs=[tok, w1_out, w2_out],
        out_shape=[jax.ShapeDtypeStruct((B, S, D), F32),
                   jax.ShapeDtypeStruct(w_ff1.shape[1:], BF16),
                   jax.ShapeDtypeStruct(w_ff2.shape[1:], BF16)],
        scratch_shapes=[
            pltpu.VMEM((B, 2, WINDOW, 2 * LANES), BF16),
            pltpu.VMEM((B, 2, WINDOW, 3 * LANES), BF16),
            pltpu.VMEM((B, 2, SUBLANES, BRANCH_W), F32),
            pltpu.VMEM((B, 2, SUBLANES, BRANCH_W), F32),
            pltpu.VMEM((B, T + SUBLANES, BRANCH_W), F32),
            pltpu.VMEM((B * NS, SUBLANES * P, LANES), F32),
            pltpu.VMEM((B * NS, SUBLANES * P, LANES), F32),
            pltpu.VMEM((B * NS, SUBLANES * P, LANES), F32),
            pltpu.VMEM((B * NS, SUBLANES * P, LANES), F32),
        ],
        compiler_params=pltpu.CompilerParams(
            dimension_semantics=("arbitrary",),
            vmem_limit_bytes=VMEM_LIMIT),
        name="mixer",
    )(*[arr for arr, _ in operands])


def _ffn_kernel(x_ref, g_ref, w1_ref, w2_ref, *rest):
    n_next = (len(rest) - 1) // 2
    o_ref = rest[n_next]
    _convert_rows(zip(rest[:n_next], rest[n_next + 1:]))
    x = x_ref[...]
    hb = _rms(x, g_ref[...], D_MODEL).astype(BF16)
    acc = x
    for c in range(D_FF // D_MODEL):
        hid = _dot(hb, w1_ref[:, c * D_MODEL:(c + 1) * D_MODEL])
        hid = jnp.square(jnp.maximum(hid, 0.0)).astype(BF16)
        acc = acc + _dot(hid, w2_ref[c * D_MODEL:(c + 1) * D_MODEL, :])
    o_ref[...] = acc


def _ffn(l, x2, g, w1, w2, next_f32):
    N, D = x2.shape
    steps = N // T_FFN
    vmem = pl.BlockSpec(memory_space=pltpu.VMEM)
    tok = pl.BlockSpec((T_FFN, D), lambda i: (i, 0))
    conv = [_row_block_specs(a.shape, l + 1, steps, lambda i: i) for a in next_f32]
    outs = pl.pallas_call(
        _ffn_kernel,
        grid=(steps,),
        in_specs=[tok, _layer_spec(g.shape, l), vmem, vmem] + [c[0] for c in conv],
        out_specs=[tok] + [c[1] for c in conv],
        out_shape=[jax.ShapeDtypeStruct((N, D), F32)]
        + [jax.ShapeDtypeStruct(a.shape[1:], BF16) for a in next_f32],
        compiler_params=pltpu.CompilerParams(
            dimension_semantics=("arbitrary",), vmem_limit_bytes=VMEM_LIMIT),
        name="ffn",
    )(x2, g, w1, w2, *next_f32)
    return outs[0], outs[1:]


def _rope_tables(seq):
    pos = jnp.arange(seq, dtype=F32)
    half = ROT_DIM // 2
    inv = ROPE_THETA ** (-jnp.arange(0, ROT_DIM, 2, dtype=F32) / ROT_DIM)
    ang = pos[:, None] * inv[None, :]
    cos, sin = jnp.cos(ang), jnp.sin(ang)
    d = np.arange(LANES) % HEAD_DIM
    f = np.arange(half)[:, None]
    e_cos = ((d[None, :] < ROT_DIM) & (d[None, :] % half == f)).astype(np.float32)
    e_a = -((d[None, :] < half) & (d[None, :] == f)).astype(np.float32)
    e_b = ((d[None, :] >= half) & (d[None, :] < ROT_DIM) & (d[None, :] - half == f)).astype(np.float32)
    expand = lambda tbl, e: jnp.dot(tbl, jnp.asarray(e), precision=lax.Precision.HIGHEST)
    return (expand(cos, e_cos) + jnp.asarray((d >= ROT_DIM).astype(np.float32))[None, :],
            expand(sin, e_a), expand(sin, e_b))


@jax.jit
def kernel(x, mem, norm_mix, norm_mem, norm_mlp, w_in, b_gate, gm_v_gain, gm_ws, gm_bs, lru_conv_w, lru_conv_b, lru_wr, lru_br, lru_wi, lru_bi, lru_lambda, swa_q_gain, swa_k_gain, swa_sinks, w_mem_kv, xa_q_gain, xa_k_gain, w_branch, w_out, w_ff1, w_ff2):
    B, S, D = x.shape
    L = DEPTH
    row = lambda a: a[:, None, :]

    eye = jnp.eye(LRU_BLOCKS, dtype=F32)
    bd = lambda w: (eye[None, :, None, :, None] * w[:, :, :, None, :]).reshape(L, BRANCH_W, BRANCH_W)
    same_heads = np.array([0, 2, 5, 7])
    swap_heads = np.array([1, 3, 4, 6])
    sinks = jnp.stack([jnp.repeat(swa_sinks[:, same_heads], WINDOW, axis=1),
                       jnp.repeat(swa_sinks[:, swap_heads], WINDOW, axis=1)], axis=1)[..., None]
    stacked = dict(
        norm_mix=row(norm_mix), gm_v_gain=row(gm_v_gain), gm_ws=gm_ws,
        gm_bias=jnp.broadcast_to(jnp.swapaxes(gm_bs, 1, 2)[:, :, :, None],
                                 (L, GM_CHUNK, GM_GROUPS, LANES)).reshape(L, GM_CHUNK, BRANCH_W),
        conv_w=lru_conv_w, conv_b=row(lru_conv_b),
        w_ri=jnp.concatenate([bd(lru_wr), bd(lru_wi)], axis=2).astype(BF16),
        lru_br=row(lru_br), lru_bi=row(lru_bi), lam=row(lru_lambda),
        qk_gain=row(jnp.concatenate([jnp.tile(swa_q_gain, (1, SWA_HEADS)) * (HEAD_DIM ** -0.5),
                                     jnp.tile(swa_k_gain, (1, SWA_KV))], axis=1)),
        sinks=sinks,
        xa_q_gain=row(xa_q_gain), b_gate=b_gate,
    )
    shared = dict(ones_tile=jnp.asarray(
        np.kron(np.eye(2, dtype=np.float32), np.ones((HEAD_DIM, HEAD_DIM), np.float32)), BF16))
    rope = _rope_tables(S)
    norm_mlp_r = row(norm_mlp)
    w_br2d = w_branch.reshape(L, N_BRANCH * BRANCH_W, D_MODEL)

    kt, vx = _mem_kv(mem, norm_mem, w_mem_kv.astype(BF16), xa_k_gain)

    mix_w = (w_in[0].astype(BF16), w_br2d[0].astype(BF16), w_out[0].astype(BF16))
    for l in range(L):
        w_in_l, w_br_l, w_out_l = mix_w
        x, w1_l, w2_l = _mixer(l, x, rope, kt, vx, w_in_l, w_br_l.reshape(N_BRANCH, BRANCH_W, D_MODEL),
                               w_out_l, stacked, shared, w_ff1, w_ff2)
        nxt = (w_in, w_br2d, w_out) if l + 1 < L else ()
        x2, mix_w = _ffn(l, x.reshape(B * S, D), norm_mlp_r, w1_l, w2_l, nxt)
        x = x2.reshape(B, S, D)
    return x
```

```python
import functools

import numpy as np
import jax
import jax.numpy as jnp
from jax import lax
from jax.experimental import pallas as pl
from jax.experimental.pallas import tpu as pltpu

D_MODEL = 1024
DEPTH = 4
N_MEM = 256
EPS = 1e-6
N_BRANCH = 4
BRANCH_W = 512
GM_GROUPS = 4
GM_CHUNK = 128
LRU_BLOCKS = 8
LRU_CONV = 4
LRU_C = 8.0
HEAD_DIM = 64
SWA_HEADS = 8
SWA_KV = 2
WINDOW = 128
ROPE_THETA = 500000.0
ROT_DIM = 16
XA_HEADS = 4
XA_DIM = 128
D_FF = 4 * D_MODEL

OFF_GM = 0
OFF_LRU = 1024
OFF_QKV = 2048
OFF_QX = 2816
OFF_GATE = 3328
D_IN = 7424

LANES = 128
SUBLANES = 8
T_MIX = 256
T_FFN = 1024
MIX_LAG = 2
SCAN_UNROLL = 4
VMEM_LIMIT = 56 * 1024 * 1024

F32 = jnp.float32
BF16 = jnp.bfloat16


def _dot(a, b):
    return jnp.dot(a, b, preferred_element_type=F32)


def _rms(x, gain_row, n):
    ms = jnp.sum(x * x, axis=-1, keepdims=True) * (1.0 / n)
    return x * lax.rsqrt(ms + EPS) * gain_row


GELU_C0 = 0.7978845608028654
GELU_C1 = GELU_C0 * 0.044715


def _gelu(x):
    hx = 0.5 * x
    return hx + hx * jnp.tanh(x * (GELU_C0 + GELU_C1 * (x * x)))


def _sigmoid(x):
    return 0.5 * jnp.tanh(0.5 * x) + 0.5


def _convert_rows(pairs):
    for src, dst in pairs:
        dst[...] = src[...].astype(BF16)


BF16_ROWS = 16


def _row_block_specs(shape, l, steps, flat_step):
    _, rows, cols = shape
    rb = rows // steps
    assert rb * steps == rows and rb % BF16_ROWS == 0, (shape, steps)
    return (pl.BlockSpec((None, rb, cols), lambda *idx: (l, flat_step(*idx), 0)),
            pl.BlockSpec((rb, cols), lambda *idx: (flat_step(*idx), 0)))


def _layer_spec(shape, l):
    nd = len(shape)
    return pl.BlockSpec((None,) + tuple(shape[1:]), lambda *_: (l,) + (0,) * (nd - 1),
                        pipeline_mode=pl.Buffered(1))


def _mem_kernel(mem_ref, nmem_ref, wkv_ref, kg_ref, kt_ref, vx_ref):
    mn = _rms(mem_ref[0], nmem_ref[0], D_MODEL).astype(BF16)
    kv = _dot(mn, wkv_ref[0])
    ones = jnp.ones((N_MEM, XA_DIM), BF16)
    for hd in range(XA_HEADS):
        kh = _rms(kv[:, hd * XA_DIM:(hd + 1) * XA_DIM], kg_ref[0], XA_DIM)
        kt_ref[0, 0, hd] = kh.T.astype(BF16)
        vh = kv[:, BRANCH_W + hd * XA_DIM:BRANCH_W + (hd + 1) * XA_DIM].astype(BF16)
        vx_ref[0, 0, hd] = jnp.concatenate([vh, ones], axis=1)


def _mem_kv(mem, norm_mem, w_mem_kv_bf, xa_k_gain):
    L, B = DEPTH, mem.shape[0]
    return pl.pallas_call(
        _mem_kernel,
        grid=(L, B),
        in_specs=[
            pl.BlockSpec((1, N_MEM, D_MODEL), lambda l, b: (b, 0, 0)),
            pl.BlockSpec((1, 1, D_MODEL), lambda l, b: (l, 0, 0)),
            pl.BlockSpec((1, D_MODEL, 2 * BRANCH_W), lambda l, b: (l, 0, 0)),
            pl.BlockSpec((1, 1, XA_DIM), lambda l, b: (l, 0, 0)),
        ],
        out_specs=[
            pl.BlockSpec((1, 1, XA_HEADS, XA_DIM, N_MEM), lambda l, b: (l, b, 0, 0, 0)),
            pl.BlockSpec((1, 1, XA_HEADS, N_MEM, 2 * XA_DIM), lambda l, b: (l, b, 0, 0, 0)),
        ],
        out_shape=[
            jax.ShapeDtypeStruct((L, B, XA_HEADS, XA_DIM, N_MEM), BF16),
            jax.ShapeDtypeStruct((L, B, XA_HEADS, N_MEM, 2 * XA_DIM), BF16),
        ],
        compiler_params=pltpu.CompilerParams(
            dimension_semantics=("arbitrary", "arbitrary")),
        name="mem_kv",
    )(mem, norm_mem.reshape(L, 1, D_MODEL), w_mem_kv_bf, xa_k_gain.reshape(L, 1, XA_DIM))


def _mixer_kernel(x_ref, cos_ref, sa_ref, sb_ref, kt_ref, vx_ref,
                  w_in_ref, w_br_ref, w_out_ref,
                  nmix_ref, gmv_ref, gmws_ref, gmb_ref,
                  cw_ref, cb_ref, wri_ref, br_ref, bi_ref, lam_ref,
                  qkg_ref, ones_ref, sink_ref, xag_ref, bg_ref, w1f_ref, w2f_ref,
                  o_ref, w1b_ref, w2b_ref,
                  kprev, vprev, xtail, hcar, xbuf, a_s, b_s, hl_s, ac_s, *, T, NB):
    t = pl.program_id(0)
    _convert_rows(((w1f_ref, w1b_ref), (w2f_ref, w2b_ref)))
    nw = T // WINDOW
    J = T // SUBLANES
    P = J + SUBLANES
    NS = BRANCH_W // LANES

    cur = lax.rem(t, 2)
    nxt = 1 - cur

    @pl.when(t == 0)
    def _():
        for bi in range(NB):
            kprev[bi, 0] = jnp.zeros((WINDOW, 2 * LANES), BF16)
            vprev[bi, 0] = jnp.zeros((WINDOW, 3 * LANES), BF16)
            xtail[bi, 0] = jnp.zeros((SUBLANES, BRANCH_W), F32)
            hcar[bi, 0] = jnp.zeros((SUBLANES, BRANCH_W), F32)

    r_i = lax.broadcasted_iota(jnp.int32, (GM_CHUNK, GM_CHUNK), 0)
    c_i = lax.broadcasted_iota(jnp.int32, (GM_CHUNK, GM_CHUNK), 1)
    causal = r_i >= c_i
    qi = lax.broadcasted_iota(jnp.int32, (WINDOW, 2 * WINDOW), 0)
    kj = lax.broadcasted_iota(jnp.int32, (WINDOW, 2 * WINDOW), 1)
    diff = qi + WINDOW - kj
    band = jnp.where(diff >= 0, diff, WINDOW) < WINDOW
    bias = jnp.where(band, 0.0, -jnp.inf)
    kj0 = kj + jnp.where(t > 0, WINDOW, 0)
    bias0 = jnp.where(kj0 >= WINDOW, bias, -jnp.inf)
    lane_w = lax.broadcasted_iota(jnp.int32, (WINDOW, LANES), 1) < HEAD_DIM
    low_half = lax.broadcasted_iota(jnp.int32, (T, LANES), 1) < HEAD_DIM
    nslot = SWA_HEADS // 2
    half = SWA_HEADS // SWA_KV // 2
    scan_out = {}

    def attend(qs, kw, vw, sink, bias_w):
        s = lax.dot_general(qs, kw, (((1,), (1,)), ((), ())), preferred_element_type=F32)
        s = s + jnp.concatenate([bias_w] * nslot, axis=0)
        m = jnp.maximum(jnp.max(s, axis=-1, keepdims=True), sink)
        pr = jnp.exp(s - m).astype(BF16)
        oe = _dot(pr, vw)
        return oe[:, :LANES] / (oe[:, LANES:] + jnp.exp(sink - m))

    def stream(bi):
        x = x_ref[bi]
        hb = _rms(x, nmix_ref[...], D_MODEL).astype(BF16)

        def proj(lo, hi):
            return _dot(hb, w_in_ref[:, lo:hi])

        def merge(b, ob, acc):
            gate = _sigmoid(
                proj(OFF_GATE + b * D_MODEL, OFF_GATE + (b + 1) * D_MODEL) + bg_ref[b:b + 1, :])
            contrib = gate * _dot(ob, w_br_ref[b])
            return contrib if acc is None else acc + contrib
        yield

        zg = _gelu(proj(OFF_GM, OFF_GM + 2 * BRANCH_W))
        yield
        u = zg[:, :BRANCH_W]
        vn = _rms(zg[:, BRANCH_W:], gmv_ref[...], BRANCH_W).astype(BF16)
        s_cols = []
        for g in range(GM_GROUPS):
            wg = jnp.where(causal, gmws_ref[g], 0.0).astype(BF16)
            vcat = jnp.concatenate(
                [vn[c * GM_CHUNK:(c + 1) * GM_CHUNK, g * LANES:(g + 1) * LANES] for c in range(nw)], axis=1)
            sg = _dot(wg, vcat)
            s_cols.append(jnp.concatenate(
                [sg[:, c * LANES:(c + 1) * LANES] for c in range(nw)], axis=0))
        s_full = jnp.concatenate(s_cols, axis=1) + jnp.concatenate([gmb_ref[...]] * nw, axis=0)
        o_gm = (u * s_full).astype(BF16)
        yield

        zl = proj(OFF_LRU, OFF_LRU + 2 * BRANCH_W)
        xb = zl[:, :BRANCH_W]
        gb = zl[:, BRANCH_W:]
        xbuf[bi, 0:SUBLANES, :] = xtail[bi, cur]
        xbuf[bi, SUBLANES:SUBLANES + T, :] = xb
        xtail[bi, nxt] = xb[T - SUBLANES:, :]
        yield
        xc = cb_ref[...] + xb * cw_ref[3:4, :]
        for k in range(LRU_CONV - 1):
            sh = LRU_CONV - 1 - k
            xc = xc + xbuf[bi, SUBLANES - sh:SUBLANES - sh + T, :] * cw_ref[k:k + 1, :]
        xcb = xc.astype(BF16)
        hw = BRANCH_W // 2
        ri = [_dot(xcb[:, h * hw:(h + 1) * hw], wri_ref[h]) for h in range(2)]
        yield
        r_gate = _sigmoid(jnp.concatenate([ri[0][:, :hw], ri[1][:, :hw]], axis=1) + br_ref[...])
        i_gate = _sigmoid(jnp.concatenate([ri[0][:, hw:], ri[1][:, hw:]], axis=1) + bi_ref[...])
        nl = -lam_ref[...]
        softplus = jnp.maximum(nl, 0.0) + jnp.log1p(jnp.exp(-jnp.abs(nl)))
        log_a = (-LRU_C) * r_gate * softplus
        a = jnp.exp(log_a)
        b_in = xc * i_gate * jnp.sqrt(1.0 - a * a)
        for s in range(SUBLANES):
            for sl in range(NS):
                a_s[bi * NS + sl, s * P:s * P + J, :] = a[s * J:(s + 1) * J, sl * LANES:(sl + 1) * LANES]
                b_s[bi * NS + sl, s * P:s * P + J, :] = b_in[s * J:(s + 1) * J, sl * LANES:(sl + 1) * LANES]
        acc = merge(0, o_gm, None)
        yield "scan"
        fin = scan_out["fin"]

        h_cols = []
        for sl in range(NS):
            k = bi * NS + sl
            hl_end, ac_end = fin[2 * k], fin[2 * k + 1]
            hin = hcar[bi, cur, 0:1, sl * LANES:(sl + 1) * LANES]
            segs = []
            for s in range(SUBLANES):
                segs.append(hl_s[k, s * P:s * P + J, :] + ac_s[k, s * P:s * P + J, :] * hin)
                hin = hl_end[s:s + 1, :] + ac_end[s:s + 1, :] * hin
            hcar[bi, nxt, 0:1, sl * LANES:(sl + 1) * LANES] = hin
            h_cols.append(jnp.concatenate(segs, axis=0))
        h_lru = jnp.concatenate(h_cols, axis=1)
        o_lru = (_gelu(gb) * h_lru).astype(BF16)
        yield
        acc = merge(1, o_lru, acc)
        yield

        zq = proj(OFF_QKV, OFF_QKV + 768)
        vv = zq[:, 640:768]
        cos_t, sa_t, sb_t = cos_ref[...], sa_ref[...], sb_ref[...]
        roped = []
        for p in range(5):
            xt = zq[:, p * LANES:(p + 1) * LANES]
            ssq = _dot((xt * xt).astype(BF16), ones_ref[...])
            xt = xt * lax.rsqrt(ssq * (1.0 / HEAD_DIM) + EPS) * qkg_ref[:, p * LANES:(p + 1) * LANES]
            roped.append(xt * cos_t + pltpu.roll(xt, LANES - ROT_DIM // 2, 1) * sa_t
                         + pltpu.roll(xt, ROT_DIM // 2, 1) * sb_t)
        yield
        q_lo = [jnp.where(low_half, roped[p], 0.0).astype(BF16) for p in range(4)]
        q_hi = [jnp.where(low_half, 0.0, roped[p]).astype(BF16) for p in range(4)]
        q_same = q_lo[:half] + q_hi[half:]
        q_swap = q_hi[:half] + q_lo[half:]
        ones_t = jnp.ones((T, LANES), BF16)
        k2 = jnp.concatenate([roped[4].astype(BF16),
                              pltpu.roll(roped[4], HEAD_DIM, 1).astype(BF16)], axis=1)
        v3 = jnp.concatenate([vv.astype(BF16), ones_t,
                              pltpu.roll(vv, HEAD_DIM, 1).astype(BF16)], axis=1)
        kext = jnp.concatenate([kprev[bi, cur], k2], axis=0)
        vext = jnp.concatenate([vprev[bi, cur], v3], axis=0)
        kprev[bi, nxt] = k2[T - WINDOW:, :]
        vprev[bi, nxt] = v3[T - WINDOW:, :]
        o_win = []
        for n in range(nw):
            rows = slice(n * WINDOW, (n + 1) * WINDOW)
            krows = slice(n * WINDOW, (n + 2) * WINDOW)
            bias_w = bias0 if n == 0 else bias
            o_same = attend(jnp.concatenate([q[rows] for q in q_same], axis=0),
                            kext[krows, :LANES], vext[krows, :2 * LANES], sink_ref[0], bias_w)
            yield
            o_swap = attend(jnp.concatenate([q[rows] for q in q_swap], axis=0),
                            kext[krows, LANES:],
                            jnp.concatenate([vext[krows, 2 * LANES:], vext[krows, LANES:2 * LANES]], axis=1),
                            sink_ref[1], bias_w)
            tiles = []
            for p in range(4):
                blk = slice(p * WINDOW, (p + 1) * WINDOW)
                lo, hi = (o_same[blk], o_swap[blk]) if p < half else (o_swap[blk], o_same[blk])
                tiles.append(jnp.where(lane_w, lo, hi))
            o_win.append(jnp.concatenate(tiles, axis=1))
            yield
        o_swa = jnp.concatenate(o_win, axis=0).astype(BF16)
        acc = merge(2, o_swa, acc)
        yield

        zx = proj(OFF_QX, OFF_QX + BRANCH_W)
        xg = xag_ref[...] * (XA_DIM ** -0.5)
        o_heads = []
        for hd in range(XA_HEADS):
            qn = _rms(zx[:, hd * XA_DIM:(hd + 1) * XA_DIM], xg, XA_DIM).astype(BF16)
            s = _dot(qn, kt_ref[bi, hd])
            pr = jnp.exp(s - jnp.max(s, axis=-1, keepdims=True)).astype(BF16)
            oe = _dot(pr, vx_ref[bi, hd])
            o_heads.append(oe[:, :XA_DIM] / oe[:, XA_DIM:])
            yield
        o_xa = jnp.concatenate(o_heads, axis=1).astype(BF16)

        acc = merge(3, o_xa, acc)
        yield
        o_ref[bi] = x + _dot(acc.astype(BF16), w_out_ref[...])

    def scan_body(j, carry):
        new = []
        for k in range(NB * NS):
            hl, ac = carry[2 * k], carry[2 * k + 1]
            aj = a_s[k, pl.ds(j, SUBLANES, stride=P), :]
            bj = b_s[k, pl.ds(j, SUBLANES, stride=P), :]
            hl = aj * hl + bj
            ac = aj * ac
            hl_s[k, pl.ds(j, SUBLANES, stride=P), :] = hl
            ac_s[k, pl.ds(j, SUBLANES, stride=P), :] = ac
            new += [hl, ac]
        return tuple(new)

    gens = [stream(bi) for bi in range(NB)]
    done = [False] * NB
    at_scan = [False] * NB
    steps = [0] * NB

    def advance(i):
        try:
            if next(gens[i]) == "scan":
                at_scan[i] = True
            steps[i] += 1
        except StopIteration:
            done[i] = True

    def run_until(pred):
        while not pred():
            for i in range(NB):
                lead_ok = i == 0 or done[i - 1] or at_scan[i - 1] or steps[i - 1] - steps[i] >= MIX_LAG
                if not done[i] and not at_scan[i] and lead_ok:
                    advance(i)

    run_until(lambda: all(at_scan))
    init = tuple(jnp.zeros((SUBLANES, LANES), F32) if i % 2 == 0 else jnp.ones((SUBLANES, LANES), F32)
                 for i in range(2 * NB * NS))
    scan_out["fin"] = lax.fori_loop(0, J, scan_body, init, unroll=SCAN_UNROLL)
    at_scan[:] = [False] * NB
    steps[:] = [0] * NB
    run_until(lambda: all(done))


def _mixer(l, x, rope, kt, vx, w_in, w_br, w_out, stacked, shared, w_ff1, w_ff2):
    B, S, D = x.shape
    T = T_MIX
    nt = S // T
    J = T // SUBLANES
    P = J + SUBLANES
    NS = BRANCH_W // LANES
    vmem = pl.BlockSpec(memory_space=pltpu.VMEM)
    tok = pl.BlockSpec((B, T, D), lambda t: (0, t, 0))
    rp = pl.BlockSpec((T, LANES), lambda t: (t, 0))
    w1_in, w1_out = _row_block_specs(w_ff1.shape, l, nt, lambda t: t)
    w2_in, w2_out = _row_block_specs(w_ff2.shape, l, nt, lambda t: t)
    s = stacked
    operands = [
        (x, tok), (rope[0], rp), (rope[1], rp), (rope[2], rp),
        (kt, pl.BlockSpec((None, B, XA_HEADS, XA_DIM, N_MEM), lambda t: (l, 0, 0, 0, 0))),
        (vx, pl.BlockSpec((None, B, XA_HEADS, N_MEM, 2 * XA_DIM), lambda t: (l, 0, 0, 0, 0))),
        (w_in, vmem), (w_br, vmem), (w_out, vmem),
    ]
    for arr in (s["norm_mix"], s["gm_v_gain"], s["gm_ws"], s["gm_bias"],
                s["conv_w"], s["conv_b"], s["w_ri"], s["lru_br"], s["lru_bi"], s["lam"], s["qk_gain"]):
        operands.append((arr, _layer_spec(arr.shape, l)))
    operands.append((shared["ones_tile"], vmem))
    for arr in (s["sinks"], s["xa_q_gain"], s["b_gate"]):
        operands.append((arr, _layer_spec(arr.shape, l)))
    operands += [(w_ff1, w1_in), (w_ff2, w2_in)]
    return pl.pallas_call(
        functools.partial(_mixer_kernel, T=T, NB=B),
        grid=(nt,),
        in_specs=[spec for _, spec in operands],
        out_specs=[tok, w1_out, w2_out],
        out_shape=[jax.ShapeDtypeStruct((B, S, D), F32),
                   jax.ShapeDtypeStruct(w_ff1.shape[1:], BF16),
                   jax.ShapeDtypeStruct(w_ff2.shape[1:], BF16)],
        scratch_shapes=[
            pltpu.VMEM((B, 2, WINDOW, 2 * LANES), BF16),
            pltpu.VMEM((B, 2, WINDOW, 3 * LANES), BF16),
            pltpu.VMEM((B, 2, SUBLANES, BRANCH_W), F32),
            pltpu.VMEM((B, 2, SUBLANES, BRANCH_W), F32),
            pltpu.VMEM((B, T + SUBLANES, BRANCH_W), F32),
            pltpu.VMEM((B * NS, SUBLANES * P, LANES), F32),
            pltpu.VMEM((B * NS, SUBLANES * P, LANES), F32),
            pltpu.VMEM((B * NS, SUBLANES * P, LANES), F32),
            pltpu.VMEM((B * NS, SUBLANES * P, LANES), F32),
        ],
        compiler_params=pltpu.CompilerParams(
            dimension_semantics=("arbitrary",),
            vmem_limit_bytes=VMEM_LIMIT),
        name="mixer",
    )(*[arr for arr, _ in operands])


def _ffn_kernel(x_ref, g_ref, w1_ref, w2_ref, *rest):
    n_next = (len(rest) - 1) // 2
    o_ref = rest[n_next]
    _convert_rows(zip(rest[:n_next], rest[n_next + 1:]))
    half = x_ref.shape[0] // 2
    for h in range(2):
        rows = slice(h * half, (h + 1) * half)
        x = x_ref[rows, :]
        hb = _rms(x, g_ref[...], D_MODEL).astype(BF16)
        acc = x
        for c in range(D_FF // D_MODEL):
            hid = _dot(hb, w1_ref[:, c * D_MODEL:(c + 1) * D_MODEL])
            hid = jnp.square(jnp.maximum(hid, 0.0)).astype(BF16)
            acc = acc + _dot(hid, w2_ref[c * D_MODEL:(c + 1) * D_MODEL, :])
        o_ref[rows, :] = acc


def _ffn(l, x2, g, w1, w2, next_f32):
    N, D = x2.shape
    steps = N // T_FFN
    vmem = pl.BlockSpec(memory_space=pltpu.VMEM)
    tok = pl.BlockSpec((T_FFN, D), lambda i: (i, 0))
    conv = [_row_block_specs(a.shape, l + 1, steps, lambda i: i) for a in next_f32]
    outs = pl.pallas_call(
        _ffn_kernel,
        grid=(steps,),
        in_specs=[tok, _layer_spec(g.shape, l), vmem, vmem] + [c[0] for c in conv],
        out_specs=[tok] + [c[1] for c in conv],
        out_shape=[jax.ShapeDtypeStruct((N, D), F32)]
        + [jax.ShapeDtypeStruct(a.shape[1:], BF16) for a in next_f32],
        compiler_params=pltpu.CompilerParams(
            dimension_semantics=("arbitrary",), vmem_limit_bytes=VMEM_LIMIT),
        name="ffn",
    )(x2, g, w1, w2, *next_f32)
    return outs[0], outs[1:]


def _rope_tables(seq):
    pos = jnp.arange(seq, dtype=F32)
    half = ROT_DIM // 2
    inv = ROPE_THETA ** (-jnp.arange(0, ROT_DIM, 2, dtype=F32) / ROT_DIM)
    ang = pos[:, None] * inv[None, :]
    cos, sin = jnp.cos(ang), jnp.sin(ang)
    d = np.arange(LANES) % HEAD_DIM
    f = np.arange(half)[:, None]
    e_cos = ((d[None, :] < ROT_DIM) & (d[None, :] % half == f)).astype(np.float32)
    e_a = -((d[None, :] < half) & (d[None, :] == f)).astype(np.float32)
    e_b = ((d[None, :] >= half) & (d[None, :] < ROT_DIM) & (d[None, :] - half == f)).astype(np.float32)
    expand = lambda tbl, e: jnp.dot(tbl, jnp.asarray(e), precision=lax.Precision.HIGHEST)
    return (expand(cos, e_cos) + jnp.asarray((d >= ROT_DIM).astype(np.float32))[None, :],
            expand(sin, e_a), expand(sin, e_b))


@jax.jit
def kernel(x, mem, norm_mix, norm_mem, norm_mlp, w_in, b_gate, gm_v_gain, gm_ws, gm_bs, lru_conv_w, lru_conv_b, lru_wr, lru_br, lru_wi, lru_bi, lru_lambda, swa_q_gain, swa_k_gain, swa_sinks, w_mem_kv, xa_q_gain, xa_k_gain, w_branch, w_out, w_ff1, w_ff2):
    B, S, D = x.shape
    L = DEPTH
    row = lambda a: a[:, None, :]

    hb_n = LRU_BLOCKS // 2
    eye = jnp.eye(hb_n, dtype=F32)
    bd = lambda w: (eye[None, None, :, None, :, None]
                    * w.reshape(L, 2, hb_n, BRANCH_W // LRU_BLOCKS, 1, BRANCH_W // LRU_BLOCKS)
                    ).reshape(L, 2, BRANCH_W // 2, BRANCH_W // 2)
    same_heads = np.array([0, 2, 5, 7])
    swap_heads = np.array([1, 3, 4, 6])
    sinks = jnp.stack([jnp.repeat(swa_sinks[:, same_heads], WINDOW, axis=1),
                       jnp.repeat(swa_sinks[:, swap_heads], WINDOW, axis=1)], axis=1)[..., None]
    stacked = dict(
        norm_mix=row(norm_mix), gm_v_gain=row(gm_v_gain), gm_ws=gm_ws,
        gm_bias=jnp.broadcast_to(jnp.swapaxes(gm_bs, 1, 2)[:, :, :, None],
                                 (L, GM_CHUNK, GM_GROUPS, LANES)).reshape(L, GM_CHUNK, BRANCH_W),
        conv_w=lru_conv_w, conv_b=row(lru_conv_b),
        w_ri=jnp.concatenate([bd(lru_wr), bd(lru_wi)], axis=3).astype(BF16),
        lru_br=row(lru_br), lru_bi=row(lru_bi), lam=row(lru_lambda),
        qk_gain=row(jnp.concatenate([jnp.tile(swa_q_gain, (1, SWA_HEADS)) * (HEAD_DIM ** -0.5),
                                     jnp.tile(swa_k_gain, (1, SWA_KV))], axis=1)),
        sinks=sinks,
        xa_q_gain=row(xa_q_gain), b_gate=b_gate,
    )
    shared = dict(ones_tile=jnp.asarray(
        np.kron(np.eye(2, dtype=np.float32), np.ones((HEAD_DIM, HEAD_DIM), np.float32)), BF16))
    rope = _rope_tables(S)
    norm_mlp_r = row(norm_mlp)
    w_br2d = w_branch.reshape(L, N_BRANCH * BRANCH_W, D_MODEL)

    kt, vx = _mem_kv(mem, norm_mem, w_mem_kv.astype(BF16), xa_k_gain)

    mix_w = (w_in[0].astype(BF16), w_br2d[0].astype(BF16), w_out[0].astype(BF16))
    for l in range(L):
        w_in_l, w_br_l, w_out_l = mix_w
        x, w1_l, w2_l = _mixer(l, x, rope, kt, vx, w_in_l, w_br_l.reshape(N_BRANCH, BRANCH_W, D_MODEL),
                               w_out_l, stacked, shared, w_ff1, w_ff2)
        nxt = (w_in, w_br2d, w_out) if l + 1 < L else ()
        x2, mix_w = _ffn(l, x.reshape(B * S, D), norm_mlp_r, w1_l, w2_l, nxt)
        x = x2.reshape(B, S, D)
    return x
```

```python
import functools

import numpy as np
import jax
import jax.numpy as jnp
from jax import lax
from jax.experimental import pallas as pl
from jax.experimental.pallas import tpu as pltpu

D_MODEL = 1024
DEPTH = 4
N_MEM = 256
EPS = 1e-6
N_BRANCH = 4
BRANCH_W = 512
GM_GROUPS = 4
GM_CHUNK = 128
LRU_BLOCKS = 8
LRU_CONV = 4
LRU_C = 8.0
HEAD_DIM = 64
SWA_HEADS = 8
SWA_KV = 2
WINDOW = 128
ROPE_THETA = 500000.0
ROT_DIM = 16
XA_HEADS = 4
XA_DIM = 128
D_FF = 4 * D_MODEL

OFF_GM = 0
OFF_LRU = 1024
OFF_QKV = 2048
OFF_QX = 2816
OFF_GATE = 3328
D_IN = 7424

LANES = 128
SUBLANES = 8
T_MIX = 256
T_FFN = 1024
MIX_LAG = 2
SCAN_UNROLL = 4
VMEM_LIMIT = 56 * 1024 * 1024

F32 = jnp.float32
BF16 = jnp.bfloat16


def _dot(a, b):
    return jnp.dot(a, b, preferred_element_type=F32)


def _rms(x, gain_row, n):
    ms = jnp.sum(x * x, axis=-1, keepdims=True) * (1.0 / n)
    return x * lax.rsqrt(ms + EPS) * gain_row


GELU_C0 = 0.7978845608028654
GELU_C1 = GELU_C0 * 0.044715


def _gelu(x):
    hx = 0.5 * x
    return hx + hx * jnp.tanh(x * (GELU_C0 + GELU_C1 * (x * x)))


def _sigmoid(x):
    return 0.5 * jnp.tanh(0.5 * x) + 0.5


def _convert_rows(pairs):
    for src, dst in pairs:
        dst[...] = src[...].astype(BF16)


BF16_ROWS = 16


def _row_block_specs(shape, l, steps, flat_step):
    _, rows, cols = shape
    rb = rows // steps
    assert rb * steps == rows and rb % BF16_ROWS == 0, (shape, steps)
    return (pl.BlockSpec((None, rb, cols), lambda *idx: (l, flat_step(*idx), 0)),
            pl.BlockSpec((rb, cols), lambda *idx: (flat_step(*idx), 0)))


def _layer_spec(shape, l):
    nd = len(shape)
    return pl.BlockSpec((None,) + tuple(shape[1:]), lambda *_: (l,) + (0,) * (nd - 1),
                        pipeline_mode=pl.Buffered(1))


def _mem_kernel(mem_ref, nmem_ref, wkv_ref, kg_ref, kt_ref, vx_ref):
    mn = _rms(mem_ref[0], nmem_ref[0], D_MODEL).astype(BF16)
    kv = _dot(mn, wkv_ref[0])
    ones = jnp.ones((N_MEM, XA_DIM), BF16)
    for hd in range(XA_HEADS):
        kh = _rms(kv[:, hd * XA_DIM:(hd + 1) * XA_DIM], kg_ref[0], XA_DIM)
        kt_ref[0, 0, hd] = kh.T.astype(BF16)
        vh = kv[:, BRANCH_W + hd * XA_DIM:BRANCH_W + (hd + 1) * XA_DIM].astype(BF16)
        vx_ref[0, 0, hd] = jnp.concatenate([vh, ones], axis=1)


def _mem_kv(mem, norm_mem, w_mem_kv_bf, xa_k_gain):
    L, B = DEPTH, mem.shape[0]
    return pl.pallas_call(
        _mem_kernel,
        grid=(L, B),
        in_specs=[
            pl.BlockSpec((1, N_MEM, D_MODEL), lambda l, b: (b, 0, 0)),
            pl.BlockSpec((1, 1, D_MODEL), lambda l, b: (l, 0, 0)),
            pl.BlockSpec((1, D_MODEL, 2 * BRANCH_W), lambda l, b: (l, 0, 0)),
            pl.BlockSpec((1, 1, XA_DIM), lambda l, b: (l, 0, 0)),
        ],
        out_specs=[
            pl.BlockSpec((1, 1, XA_HEADS, XA_DIM, N_MEM), lambda l, b: (l, b, 0, 0, 0)),
            pl.BlockSpec((1, 1, XA_HEADS, N_MEM, 2 * XA_DIM), lambda l, b: (l, b, 0, 0, 0)),
        ],
        out_shape=[
            jax.ShapeDtypeStruct((L, B, XA_HEADS, XA_DIM, N_MEM), BF16),
            jax.ShapeDtypeStruct((L, B, XA_HEADS, N_MEM, 2 * XA_DIM), BF16),
        ],
        compiler_params=pltpu.CompilerParams(
            dimension_semantics=("arbitrary", "arbitrary")),
        name="mem_kv",
    )(mem, norm_mem.reshape(L, 1, D_MODEL), w_mem_kv_bf, xa_k_gain.reshape(L, 1, XA_DIM))


def _mixer_kernel(x_ref, cos_ref, sa_ref, sb_ref, kt_ref, vx_ref,
                  w_in_ref, w_br_ref, w_out_ref,
                  nmix_ref, gmv_ref, gmws_ref, gmb_ref,
                  cw_ref, cb_ref, wri_ref, br_ref, bi_ref, lam_ref,
                  qkg_ref, ones_ref, sink_ref, xag_ref, bg_ref, w1f_ref, w2f_ref,
                  o_ref, w1b_ref, w2b_ref,
                  kprev, vprev, xtail, hcar, xbuf, a_s, b_s, hl_s, ac_s, *, T, NB):
    t = pl.program_id(0)
    _convert_rows(((w1f_ref, w1b_ref), (w2f_ref, w2b_ref)))
    nw = T // WINDOW
    J = T // SUBLANES
    P = J + SUBLANES
    NS = BRANCH_W // LANES

    cur = lax.rem(t, 2)
    nxt = 1 - cur

    @pl.when(t == 0)
    def _():
        for bi in range(NB):
            kprev[bi, 0] = jnp.zeros((WINDOW, 2 * LANES), BF16)
            vprev[bi, 0] = jnp.zeros((WINDOW, 3 * LANES), BF16)
            xtail[bi, 0] = jnp.zeros((SUBLANES, BRANCH_W), F32)
            hcar[bi, 0] = jnp.zeros((SUBLANES, BRANCH_W), F32)

    r_i = lax.broadcasted_iota(jnp.int32, (GM_CHUNK, GM_CHUNK), 0)
    c_i = lax.broadcasted_iota(jnp.int32, (GM_CHUNK, GM_CHUNK), 1)
    causal = r_i >= c_i
    qi = lax.broadcasted_iota(jnp.int32, (WINDOW, 2 * WINDOW), 0)
    kj = lax.broadcasted_iota(jnp.int32, (WINDOW, 2 * WINDOW), 1)
    diff = qi + WINDOW - kj
    band = jnp.where(diff >= 0, diff, WINDOW) < WINDOW
    bias = jnp.where(band, 0.0, -jnp.inf)
    kj0 = kj + jnp.where(t > 0, WINDOW, 0)
    bias0 = jnp.where(kj0 >= WINDOW, bias, -jnp.inf)
    lane_w = lax.broadcasted_iota(jnp.int32, (WINDOW, LANES), 1) < HEAD_DIM
    low_half = lax.broadcasted_iota(jnp.int32, (T, LANES), 1) < HEAD_DIM
    nslot = SWA_HEADS // 2
    half = SWA_HEADS // SWA_KV // 2
    scan_out = {}

    def scores(qs, kw, sink, bias_w):
        s = lax.dot_general(qs, kw, (((1,), (1,)), ((), ())), preferred_element_type=F32)
        s = s + jnp.concatenate([bias_w] * nslot, axis=0)
        m = jnp.maximum(jnp.max(s, axis=-1, keepdims=True), sink)
        return jnp.exp(s - m).astype(BF16), jnp.exp(sink - m)

    def values(pr, tail, vw):
        oe = _dot(pr, vw)
        return oe[:, :LANES] / (oe[:, LANES:] + tail)

    def stream(bi):
        x = x_ref[bi]
        hb = _rms(x, nmix_ref[...], D_MODEL).astype(BF16)

        def proj(lo, hi):
            return _dot(hb, w_in_ref[:, lo:hi])

        def merge(b, ob, acc):
            gate = _sigmoid(
                proj(OFF_GATE + b * D_MODEL, OFF_GATE + (b + 1) * D_MODEL) + bg_ref[b:b + 1, :])
            contrib = gate * _dot(ob, w_br_ref[b])
            return contrib if acc is None else acc + contrib
        yield

        zg = _gelu(proj(OFF_GM, OFF_GM + 2 * BRANCH_W))
        yield
        u = zg[:, :BRANCH_W]
        vn = _rms(zg[:, BRANCH_W:], gmv_ref[...], BRANCH_W).astype(BF16)
        s_cols = []
        for g in range(GM_GROUPS):
            wg = jnp.where(causal, gmws_ref[g], 0.0).astype(BF16)
            vcat = jnp.concatenate(
                [vn[c * GM_CHUNK:(c + 1) * GM_CHUNK, g * LANES:(g + 1) * LANES] for c in range(nw)], axis=1)
            sg = _dot(wg, vcat)
            s_cols.append(jnp.concatenate(
                [sg[:, c * LANES:(c + 1) * LANES] for c in range(nw)], axis=0))
        s_full = jnp.concatenate(s_cols, axis=1) + jnp.concatenate([gmb_ref[...]] * nw, axis=0)
        o_gm = (u * s_full).astype(BF16)
        yield

        zl = proj(OFF_LRU, OFF_LRU + 2 * BRANCH_W)
        xb = zl[:, :BRANCH_W]
        gb = zl[:, BRANCH_W:]
        xbuf[bi, 0:SUBLANES, :] = xtail[bi, cur]
        xbuf[bi, SUBLANES:SUBLANES + T, :] = xb
        xtail[bi, nxt] = xb[T - SUBLANES:, :]
        yield
        xc = cb_ref[...] + xb * cw_ref[3:4, :]
        for k in range(LRU_CONV - 1):
            sh = LRU_CONV - 1 - k
            xc = xc + xbuf[bi, SUBLANES - sh:SUBLANES - sh + T, :] * cw_ref[k:k + 1, :]
        xcb = xc.astype(BF16)
        hw = BRANCH_W // 2
        ri = [_dot(xcb[:, h * hw:(h + 1) * hw], wri_ref[h]) for h in range(2)]
        yield
        r_gate = _sigmoid(jnp.concatenate([ri[0][:, :hw], ri[1][:, :hw]], axis=1) + br_ref[...])
        i_gate = _sigmoid(jnp.concatenate([ri[0][:, hw:], ri[1][:, hw:]], axis=1) + bi_ref[...])
        nl = -lam_ref[...]
        softplus = jnp.maximum(nl, 0.0) + jnp.log1p(jnp.exp(-jnp.abs(nl)))
        log_a = (-LRU_C) * r_gate * softplus
        a = jnp.exp(log_a)
        b_in = xc * i_gate * jnp.sqrt(1.0 - a * a)
        for s in range(SUBLANES):
            for sl in range(NS):
                a_s[bi * NS + sl, s * P:s * P + J, :] = a[s * J:(s + 1) * J, sl * LANES:(sl + 1) * LANES]
                b_s[bi * NS + sl, s * P:s * P + J, :] = b_in[s * J:(s + 1) * J, sl * LANES:(sl + 1) * LANES]
        acc = merge(0, o_gm, None)
        yield "scan"
        fin = scan_out["fin"]

        h_cols = []
        for sl in range(NS):
            k = bi * NS + sl
            hl_end, ac_end = fin[2 * k], fin[2 * k + 1]
            hin = hcar[bi, cur, 0:1, sl * LANES:(sl + 1) * LANES]
            segs = []
            for s in range(SUBLANES):
                segs.append(hl_s[k, s * P:s * P + J, :] + ac_s[k, s * P:s * P + J, :] * hin)
                hin = hl_end[s:s + 1, :] + ac_end[s:s + 1, :] * hin
            hcar[bi, nxt, 0:1, sl * LANES:(sl + 1) * LANES] = hin
            h_cols.append(jnp.concatenate(segs, axis=0))
        h_lru = jnp.concatenate(h_cols, axis=1)
        o_lru = (_gelu(gb) * h_lru).astype(BF16)
        yield
        acc = merge(1, o_lru, acc)
        yield

        zq = proj(OFF_QKV, OFF_QKV + 768)
        vv = zq[:, 640:768]
        cos_t, sa_t, sb_t = cos_ref[...], sa_ref[...], sb_ref[...]
        roped = []
        for p in range(5):
            xt = zq[:, p * LANES:(p + 1) * LANES]
            ssq = _dot((xt * xt).astype(BF16), ones_ref[...])
            xt = xt * lax.rsqrt(ssq * (1.0 / HEAD_DIM) + EPS) * qkg_ref[:, p * LANES:(p + 1) * LANES]
            roped.append(xt * cos_t + pltpu.roll(xt, LANES - ROT_DIM // 2, 1) * sa_t
                         + pltpu.roll(xt, ROT_DIM // 2, 1) * sb_t)
        yield
        q_lo = [jnp.where(low_half, roped[p], 0.0).astype(BF16) for p in range(4)]
        q_hi = [jnp.where(low_half, 0.0, roped[p]).astype(BF16) for p in range(4)]
        q_same = q_lo[:half] + q_hi[half:]
        q_swap = q_hi[:half] + q_lo[half:]
        ones_t = jnp.ones((T, LANES), BF16)
        k2 = jnp.concatenate([roped[4].astype(BF16),
                              pltpu.roll(roped[4], HEAD_DIM, 1).astype(BF16)], axis=1)
        v3 = jnp.concatenate([vv.astype(BF16), ones_t,
                              pltpu.roll(vv, HEAD_DIM, 1).astype(BF16)], axis=1)
        kext = jnp.concatenate([kprev[bi, cur], k2], axis=0)
        vext = jnp.concatenate([vprev[bi, cur], v3], axis=0)
        kprev[bi, nxt] = k2[T - WINDOW:, :]
        vprev[bi, nxt] = v3[T - WINDOW:, :]
        probs = []
        for n in range(nw):
            rows = slice(n * WINDOW, (n + 1) * WINDOW)
            krows = slice(n * WINDOW, (n + 2) * WINDOW)
            bias_w = bias0 if n == 0 else bias
            probs.append((scores(jnp.concatenate([q[rows] for q in q_same], axis=0),
                                 kext[krows, :LANES], sink_ref[0], bias_w),
                          scores(jnp.concatenate([q[rows] for q in q_swap], axis=0),
                                 kext[krows, LANES:], sink_ref[1], bias_w)))
            yield
        o_win = []
        for n in range(nw):
            krows = slice(n * WINDOW, (n + 2) * WINDOW)
            (p_same, t_same), (p_swap, t_swap) = probs[n]
            o_same = values(p_same, t_same, vext[krows, :2 * LANES])
            o_swap = values(p_swap, t_swap,
                            jnp.concatenate([vext[krows, 2 * LANES:], vext[krows, LANES:2 * LANES]], axis=1))
            tiles = []
            for p in range(4):
                blk = slice(p * WINDOW, (p + 1) * WINDOW)
                lo, hi = (o_same[blk], o_swap[blk]) if p < half else (o_swap[blk], o_same[blk])
                tiles.append(jnp.where(lane_w, lo, hi))
            o_win.append(jnp.concatenate(tiles, axis=1))
            yield
        o_swa = jnp.concatenate(o_win, axis=0).astype(BF16)
        acc = merge(2, o_swa, acc)
        yield

        zx = proj(OFF_QX, OFF_QX + BRANCH_W)
        xg = xag_ref[...] * (XA_DIM ** -0.5)
        prs = []
        for hd in range(XA_HEADS):
            qn = _rms(zx[:, hd * XA_DIM:(hd + 1) * XA_DIM], xg, XA_DIM).astype(BF16)
            s = _dot(qn, kt_ref[bi, hd])
            prs.append(jnp.exp(s - jnp.max(s, axis=-1, keepdims=True)).astype(BF16))
            if hd % 2:
                yield
        o_heads = []
        for hd in range(XA_HEADS):
            oe = _dot(prs[hd], vx_ref[bi, hd])
            o_heads.append(oe[:, :XA_DIM] / oe[:, XA_DIM:])
            if hd % 2:
                yield
        o_xa = jnp.concatenate(o_heads, axis=1).astype(BF16)

        acc = merge(3, o_xa, acc)
        yield
        o_ref[bi] = x + _dot(acc.astype(BF16), w_out_ref[...])

    def scan_body(j, carry):
        new = []
        for k in range(NB * NS):
            hl, ac = carry[2 * k], carry[2 * k + 1]
            aj = a_s[k, pl.ds(j, SUBLANES, stride=P), :]
            bj = b_s[k, pl.ds(j, SUBLANES, stride=P), :]
            hl = aj * hl + bj
            ac = aj * ac
            hl_s[k, pl.ds(j, SUBLANES, stride=P), :] = hl
            ac_s[k, pl.ds(j, SUBLANES, stride=P), :] = ac
            new += [hl, ac]
        return tuple(new)

    gens = [stream(bi) for bi in range(NB)]
    done = [False] * NB
    at_scan = [False] * NB
    steps = [0] * NB

    def advance(i):
        try:
            if next(gens[i]) == "scan":
                at_scan[i] = True
            steps[i] += 1
        except StopIteration:
            done[i] = True

    def run_until(pred):
        while not pred():
            for i in range(NB):
                lead_ok = i == 0 or done[i - 1] or at_scan[i - 1] or steps[i - 1] - steps[i] >= MIX_LAG
                if not done[i] and not at_scan[i] and lead_ok:
                    advance(i)

    run_until(lambda: all(at_scan))
    init = tuple(jnp.zeros((SUBLANES, LANES), F32) if i % 2 == 0 else jnp.ones((SUBLANES, LANES), F32)
                 for i in range(2 * NB * NS))
    scan_out["fin"] = lax.fori_loop(0, J, scan_body, init, unroll=SCAN_UNROLL)
    at_scan[:] = [False] * NB
    steps[:] = [0] * NB
    run_until(lambda: all(done))


def _mixer(l, x, rope, kt, vx, w_in, w_br, w_out, stacked, shared, w_ff1, w_ff2):
    B, S, D = x.shape
    T = T_MIX
    nt = S // T
    J = T // SUBLANES
    P = J + SUBLANES
    NS = BRANCH_W // LANES
    vmem = pl.BlockSpec(memory_space=pltpu.VMEM)
    tok = pl.BlockSpec((B, T, D), lambda t: (0, t, 0))
    rp = pl.BlockSpec((T, LANES), lambda t: (t, 0))
    w1_in, w1_out = _row_block_specs(w_ff1.shape, l, nt, lambda t: t)
    w2_in, w2_out = _row_block_specs(w_ff2.shape, l, nt, lambda t: t)
    s = stacked
    operands = [
        (x, tok), (rope[0], rp), (rope[1], rp), (rope[2], rp),
        (kt, pl.BlockSpec((None, B, XA_HEADS, XA_DIM, N_MEM), lambda t: (l, 0, 0, 0, 0))),
        (vx, pl.BlockSpec((None, B, XA_HEADS, N_MEM, 2 * XA_DIM), lambda t: (l, 0, 0, 0, 0))),
        (w_in, vmem), (w_br, vmem), (w_out, vmem),
    ]
    for arr in (s["norm_mix"], s["gm_v_gain"], s["gm_ws"], s["gm_bias"],
                s["conv_w"], s["conv_b"], s["w_ri"], s["lru_br"], s["lru_bi"], s["lam"], s["qk_gain"]):
        operands.append((arr, _layer_spec(arr.shape, l)))
    operands.append((shared["ones_tile"], vmem))
    for arr in (s["sinks"], s["xa_q_gain"], s["b_gate"]):
        operands.append((arr, _layer_spec(arr.shape, l)))
    operands += [(w_ff1, w1_in), (w_ff2, w2_in)]
    return pl.pallas_call(
        functools.partial(_mixer_kernel, T=T, NB=B),
        grid=(nt,),
        in_specs=[spec for _, spec in operands],
        out_specs=[tok, w1_out, w2_out],
        out_shape=[jax.ShapeDtypeStruct((B, S, D), F32),
                   jax.ShapeDtypeStruct(w_ff1.shape[1:], BF16),
                   jax.ShapeDtypeStruct(w_ff2.shape[1:], BF16)],
        scratch_shapes=[
            pltpu.VMEM((B, 2, WINDOW, 2 * LANES), BF16),
            pltpu.VMEM((B, 2, WINDOW, 3 * LANES), BF16),
            pltpu.VMEM((B, 2, SUBLANES, BRANCH_W), F32),
            pltpu.VMEM((B, 2, SUBLANES, BRANCH_W), F32),
            pltpu.VMEM((B, T + SUBLANES, BRANCH_W), F32),
            pltpu.VMEM((B * NS, SUBLANES * P, LANES), F32),
            pltpu.VMEM((B * NS, SUBLANES * P, LANES), F32),
            pltpu.VMEM((B * NS, SUBLANES * P, LANES), F32),
            pltpu.VMEM((B * NS, SUBLANES * P, LANES), F32),
        ],
        compiler_params=pltpu.CompilerParams(
            dimension_semantics=("arbitrary",),
            vmem_limit_bytes=VMEM_LIMIT),
        name="mixer",
    )(*[arr for arr, _ in operands])


def _ffn_kernel(x_ref, g_ref, w1_ref, w2_ref, *rest):
    n_next = (len(rest) - 1) // 2
    o_ref = rest[n_next]
    _convert_rows(zip(rest[:n_next], rest[n_next + 1:]))
    half = x_ref.shape[0] // 2
    for h in range(2):
        rows = slice(h * half, (h + 1) * half)
        x = x_ref[rows, :]
        hb = _rms(x, g_ref[...], D_MODEL).astype(BF16)
        acc = x
        for c in range(D_FF // D_MODEL):
            hid = _dot(hb, w1_ref[:, c * D_MODEL:(c + 1) * D_MODEL])
            hid = jnp.square(jnp.maximum(hid, 0.0)).astype(BF16)
            acc = acc + _dot(hid, w2_ref[c * D_MODEL:(c + 1) * D_MODEL, :])
        o_ref[rows, :] = acc


def _ffn(l, x2, g, w1, w2, next_f32):
    N, D = x2.shape
    steps = N // T_FFN
    vmem = pl.BlockSpec(memory_space=pltpu.VMEM)
    tok = pl.BlockSpec((T_FFN, D), lambda i: (i, 0))
    conv = [_row_block_specs(a.shape, l + 1, steps, lambda i: i) for a in next_f32]
    outs = pl.pallas_call(
        _ffn_kernel,
        grid=(steps,),
        in_specs=[tok, _layer_spec(g.shape, l), vmem, vmem] + [c[0] for c in conv],
        out_specs=[tok] + [c[1] for c in conv],
        out_shape=[jax.ShapeDtypeStruct((N, D), F32)]
        + [jax.ShapeDtypeStruct(a.shape[1:], BF16) for a in next_f32],
        compiler_params=pltpu.CompilerParams(
            dimension_semantics=("arbitrary",), vmem_limit_bytes=VMEM_LIMIT),
        name="ffn",
    )(x2, g, w1, w2, *next_f32)
    return outs[0], outs[1:]


def _rope_tables(seq):
    pos = jnp.arange(seq, dtype=F32)
    half = ROT_DIM // 2
    inv = ROPE_THETA ** (-jnp.arange(0, ROT_DIM, 2, dtype=F32) / ROT_DIM)
    ang = pos[:, None] * inv[None, :]
    cos, sin = jnp.cos(ang), jnp.sin(ang)
    d = np.arange(LANES) % HEAD_DIM
    f = np.arange(half)[:, None]
    e_cos = ((d[None, :] < ROT_DIM) & (d[None, :] % half == f)).astype(np.float32)
    e_a = -((d[None, :] < half) & (d[None, :] == f)).astype(np.float32)
    e_b = ((d[None, :] >= half) & (d[None, :] < ROT_DIM) & (d[None, :] - half == f)).astype(np.float32)
    expand = lambda tbl, e: jnp.dot(tbl, jnp.asarray(e), precision=lax.Precision.HIGHEST)
    return (expand(cos, e_cos) + jnp.asarray((d >= ROT_DIM).astype(np.float32))[None, :],
            expand(sin, e_a), expand(sin, e_b))


@jax.jit
def kernel(x, mem, norm_mix, norm_mem, norm_mlp, w_in, b_gate, gm_v_gain, gm_ws, gm_bs, lru_conv_w, lru_conv_b, lru_wr, lru_br, lru_wi, lru_bi, lru_lambda, swa_q_gain, swa_k_gain, swa_sinks, w_mem_kv, xa_q_gain, xa_k_gain, w_branch, w_out, w_ff1, w_ff2):
    B, S, D = x.shape
    L = DEPTH
    row = lambda a: a[:, None, :]

    hb_n = LRU_BLOCKS // 2
    eye = jnp.eye(hb_n, dtype=F32)
    bd = lambda w: (eye[None, None, :, None, :, None]
                    * w.reshape(L, 2, hb_n, BRANCH_W // LRU_BLOCKS, 1, BRANCH_W // LRU_BLOCKS)
                    ).reshape(L, 2, BRANCH_W // 2, BRANCH_W // 2)
    same_heads = np.array([0, 2, 5, 7])
    swap_heads = np.array([1, 3, 4, 6])
    sinks = jnp.stack([jnp.repeat(swa_sinks[:, same_heads], WINDOW, axis=1),
                       jnp.repeat(swa_sinks[:, swap_heads], WINDOW, axis=1)], axis=1)[..., None]
    stacked = dict(
        norm_mix=row(norm_mix), gm_v_gain=row(gm_v_gain), gm_ws=gm_ws,
        gm_bias=jnp.broadcast_to(jnp.swapaxes(gm_bs, 1, 2)[:, :, :, None],
                                 (L, GM_CHUNK, GM_GROUPS, LANES)).reshape(L, GM_CHUNK, BRANCH_W),
        conv_w=lru_conv_w, conv_b=row(lru_conv_b),
        w_ri=jnp.concatenate([bd(lru_wr), bd(lru_wi)], axis=3).astype(BF16),
        lru_br=row(lru_br), lru_bi=row(lru_bi), lam=row(lru_lambda),
        qk_gain=row(jnp.concatenate([jnp.tile(swa_q_gain, (1, SWA_HEADS)) * (HEAD_DIM ** -0.5),
                                     jnp.tile(swa_k_gain, (1, SWA_KV))], axis=1)),
        sinks=sinks,
        xa_q_gain=row(xa_q_gain), b_gate=b_gate,
    )
    shared = dict(ones_tile=jnp.asarray(
        np.kron(np.eye(2, dtype=np.float32), np.ones((HEAD_DIM, HEAD_DIM), np.float32)), BF16))
    rope = _rope_tables(S)
    norm_mlp_r = row(norm_mlp)
    w_br2d = w_branch.reshape(L, N_BRANCH * BRANCH_W, D_MODEL)

    kt, vx = _mem_kv(mem, norm_mem, w_mem_kv.astype(BF16), xa_k_gain)

    mix_w = (w_in[0].astype(BF16), w_br2d[0].astype(BF16), w_out[0].astype(BF16))
    for l in range(L):
        w_in_l, w_br_l, w_out_l = mix_w
        x, w1_l, w2_l = _mixer(l, x, rope, kt, vx, w_in_l, w_br_l.reshape(N_BRANCH, BRANCH_W, D_MODEL),
                               w_out_l, stacked, shared, w_ff1, w_ff2)
        nxt = (w_in, w_br2d, w_out) if l + 1 < L else ()
        x2, mix_w = _ffn(l, x.reshape(B * S, D), norm_mlp_r, w1_l, w2_l, nxt)
        x = x2.reshape(B, S, D)
    return x
```

```python
import functools

import numpy as np
import jax
import jax.numpy as jnp
from jax import lax
from jax.experimental import pallas as pl
from jax.experimental.pallas import tpu as pltpu

D_MODEL = 1024
DEPTH = 4
N_MEM = 256
EPS = 1e-6
N_BRANCH = 4
BRANCH_W = 512
GM_GROUPS = 4
GM_CHUNK = 128
LRU_BLOCKS = 8
LRU_CONV = 4
LRU_C = 8.0
HEAD_DIM = 64
SWA_HEADS = 8
SWA_KV = 2
WINDOW = 128
ROPE_THETA = 500000.0
ROT_DIM = 16
XA_HEADS = 4
XA_DIM = 128
D_FF = 4 * D_MODEL

OFF_GM = 0
OFF_LRU = 1024
OFF_QKV = 2048
OFF_QX = 2816
OFF_GATE = 3328
D_IN = 7424

LANES = 128
SUBLANES = 8
T_MIX = 256
T_FFN = 1024
MIX_LAG = 2
SCAN_UNROLL = 4
VMEM_LIMIT = 56 * 1024 * 1024

F32 = jnp.float32
BF16 = jnp.bfloat16


def _dot(a, b):
    return jnp.dot(a, b, preferred_element_type=F32)


def _rms(x, gain_row, n):
    ms = jnp.sum(x * x, axis=-1, keepdims=True) * (1.0 / n)
    return x * lax.rsqrt(ms + EPS) * gain_row


GELU_C0 = 0.7978845608028654
GELU_C1 = GELU_C0 * 0.044715


def _gelu(x):
    hx = 0.5 * x
    return hx + hx * jnp.tanh(x * (GELU_C0 + GELU_C1 * (x * x)))


def _sigmoid(x):
    return 0.5 * jnp.tanh(0.5 * x) + 0.5


def _convert_rows(pairs):
    for src, dst in pairs:
        dst[...] = src[...].astype(BF16)


BF16_ROWS = 16


def _row_block_specs(shape, l, steps, flat_step):
    _, rows, cols = shape
    rb = rows // steps
    assert rb * steps == rows and rb % BF16_ROWS == 0, (shape, steps)
    return (pl.BlockSpec((None, rb, cols), lambda *idx: (l, flat_step(*idx), 0)),
            pl.BlockSpec((rb, cols), lambda *idx: (flat_step(*idx), 0)))


def _layer_spec(shape, l):
    nd = len(shape)
    return pl.BlockSpec((None,) + tuple(shape[1:]), lambda *_: (l,) + (0,) * (nd - 1),
                        pipeline_mode=pl.Buffered(1))


def _mem_kernel(mem_ref, nmem_ref, wkv_ref, kg_ref, kt_ref, vx_ref):
    mn = _rms(mem_ref[0], nmem_ref[0], D_MODEL).astype(BF16)
    kv = _dot(mn, wkv_ref[0])
    ones = jnp.ones((N_MEM, XA_DIM), BF16)
    for hd in range(XA_HEADS):
        kh = _rms(kv[:, hd * XA_DIM:(hd + 1) * XA_DIM], kg_ref[0], XA_DIM)
        kt_ref[0, 0, hd] = kh.T.astype(BF16)
        vh = kv[:, BRANCH_W + hd * XA_DIM:BRANCH_W + (hd + 1) * XA_DIM].astype(BF16)
        vx_ref[0, 0, hd] = jnp.concatenate([vh, ones], axis=1)


def _mem_kv(mem, norm_mem, w_mem_kv_bf, xa_k_gain):
    L, B = DEPTH, mem.shape[0]
    return pl.pallas_call(
        _mem_kernel,
        grid=(L, B),
        in_specs=[
            pl.BlockSpec((1, N_MEM, D_MODEL), lambda l, b: (b, 0, 0)),
            pl.BlockSpec((1, 1, D_MODEL), lambda l, b: (l, 0, 0)),
            pl.BlockSpec((1, D_MODEL, 2 * BRANCH_W), lambda l, b: (l, 0, 0)),
            pl.BlockSpec((1, 1, XA_DIM), lambda l, b: (l, 0, 0)),
        ],
        out_specs=[
            pl.BlockSpec((1, 1, XA_HEADS, XA_DIM, N_MEM), lambda l, b: (l, b, 0, 0, 0)),
            pl.BlockSpec((1, 1, XA_HEADS, N_MEM, 2 * XA_DIM), lambda l, b: (l, b, 0, 0, 0)),
        ],
        out_shape=[
            jax.ShapeDtypeStruct((L, B, XA_HEADS, XA_DIM, N_MEM), BF16),
            jax.ShapeDtypeStruct((L, B, XA_HEADS, N_MEM, 2 * XA_DIM), BF16),
        ],
        compiler_params=pltpu.CompilerParams(
            dimension_semantics=("arbitrary", "arbitrary")),
        name="mem_kv",
    )(mem, norm_mem.reshape(L, 1, D_MODEL), w_mem_kv_bf, xa_k_gain.reshape(L, 1, XA_DIM))


def _mixer_kernel(x_ref, cos_ref, sa_ref, sb_ref, kt_ref, vx_ref,
                  w_in_ref, w_br_ref, w_out_ref,
                  nmix_ref, gmv_ref, gmws_ref, gmb_ref,
                  cw_ref, cb_ref, wri_ref, br_ref, bi_ref, lam_ref,
                  qkg_ref, ones_ref, sink_ref, xag_ref, bg_ref, w1f_ref, w2f_ref,
                  o_ref, w1b_ref, w2b_ref,
                  kprev, vprev, xtail, hcar, xbuf, a_s, b_s, hl_s, ac_s, *, T, NB):
    t = pl.program_id(0)
    _convert_rows(((w1f_ref, w1b_ref), (w2f_ref, w2b_ref)))
    nw = T // WINDOW
    J = T // SUBLANES
    P = J + SUBLANES
    NS = BRANCH_W // LANES

    cur = lax.rem(t, 2)
    nxt = 1 - cur

    @pl.when(t == 0)
    def _():
        for bi in range(NB):
            kprev[bi, 0] = jnp.zeros((WINDOW, 2 * LANES), BF16)
            vprev[bi, 0] = jnp.zeros((WINDOW, 3 * LANES), BF16)
            xtail[bi, 0] = jnp.zeros((SUBLANES, BRANCH_W), F32)
            hcar[bi, 0] = jnp.zeros((SUBLANES, BRANCH_W), F32)

    r_i = lax.broadcasted_iota(jnp.int32, (GM_CHUNK, GM_CHUNK), 0)
    c_i = lax.broadcasted_iota(jnp.int32, (GM_CHUNK, GM_CHUNK), 1)
    causal = r_i >= c_i
    qi = lax.broadcasted_iota(jnp.int32, (WINDOW, 2 * WINDOW), 0)
    kj = lax.broadcasted_iota(jnp.int32, (WINDOW, 2 * WINDOW), 1)
    diff = qi + WINDOW - kj
    band = jnp.where(diff >= 0, diff, WINDOW) < WINDOW
    bias = jnp.where(band, 0.0, -jnp.inf)
    kj0 = kj + jnp.where(t > 0, WINDOW, 0)
    bias0 = jnp.where(kj0 >= WINDOW, bias, -jnp.inf)
    lane_w = lax.broadcasted_iota(jnp.int32, (WINDOW, LANES), 1) < HEAD_DIM
    low_half = lax.broadcasted_iota(jnp.int32, (T, LANES), 1) < HEAD_DIM
    nslot = SWA_HEADS // 2
    half = SWA_HEADS // SWA_KV // 2
    scan_out = {}

    def scores(qs, kw, sink, bias_w):
        s = lax.dot_general(qs, kw, (((1,), (1,)), ((), ())), preferred_element_type=F32)
        s = s + jnp.concatenate([bias_w] * nslot, axis=0)
        m = jnp.maximum(jnp.max(s, axis=-1, keepdims=True), sink)
        return jnp.exp(s - m).astype(BF16), jnp.exp(sink - m)

    def values(pr, tail, vw):
        oe = _dot(pr, vw)
        return oe[:, :LANES] / (oe[:, LANES:] + tail)

    def stream(bi):
        x = x_ref[bi]
        hb = _rms(x, nmix_ref[...], D_MODEL).astype(BF16)

        def proj(lo, hi):
            return _dot(hb, w_in_ref[:, lo:hi])

        def merge(b, ob, acc):
            gate = _sigmoid(
                proj(OFF_GATE + b * D_MODEL, OFF_GATE + (b + 1) * D_MODEL) + bg_ref[b:b + 1, :])
            contrib = gate * _dot(ob, w_br_ref[b])
            return contrib if acc is None else acc + contrib
        yield

        zg_raw = proj(OFF_GM, OFF_GM + 2 * BRANCH_W)
        zl = proj(OFF_LRU, OFF_LRU + 2 * BRANCH_W)
        yield
        zg = _gelu(zg_raw)
        u = zg[:, :BRANCH_W]
        vn = _rms(zg[:, BRANCH_W:], gmv_ref[...], BRANCH_W).astype(BF16)
        s_cols = []
        for g in range(GM_GROUPS):
            wg = jnp.where(causal, gmws_ref[g], 0.0).astype(BF16)
            vcat = jnp.concatenate(
                [vn[c * GM_CHUNK:(c + 1) * GM_CHUNK, g * LANES:(g + 1) * LANES] for c in range(nw)], axis=1)
            sg = _dot(wg, vcat)
            s_cols.append(jnp.concatenate(
                [sg[:, c * LANES:(c + 1) * LANES] for c in range(nw)], axis=0))
        s_full = jnp.concatenate(s_cols, axis=1) + jnp.concatenate([gmb_ref[...]] * nw, axis=0)
        o_gm = (u * s_full).astype(BF16)
        yield

        xb = zl[:, :BRANCH_W]
        gb = zl[:, BRANCH_W:]
        xbuf[bi, 0:SUBLANES, :] = xtail[bi, cur]
        xbuf[bi, SUBLANES:SUBLANES + T, :] = xb
        xtail[bi, nxt] = xb[T - SUBLANES:, :]
        yield
        xc = cb_ref[...] + xb * cw_ref[3:4, :]
        for k in range(LRU_CONV - 1):
            sh = LRU_CONV - 1 - k
            xc = xc + xbuf[bi, SUBLANES - sh:SUBLANES - sh + T, :] * cw_ref[k:k + 1, :]
        xcb = xc.astype(BF16)
        hw = BRANCH_W // 2
        ri = [_dot(xcb[:, h * hw:(h + 1) * hw], wri_ref[h]) for h in range(2)]
        yield
        r_gate = _sigmoid(jnp.concatenate([ri[0][:, :hw], ri[1][:, :hw]], axis=1) + br_ref[...])
        i_gate = _sigmoid(jnp.concatenate([ri[0][:, hw:], ri[1][:, hw:]], axis=1) + bi_ref[...])
        nl = -lam_ref[...]
        softplus = jnp.maximum(nl, 0.0) + jnp.log1p(jnp.exp(-jnp.abs(nl)))
        log_a = (-LRU_C) * r_gate * softplus
        a = jnp.exp(log_a)
        b_in = xc * i_gate * jnp.sqrt(1.0 - a * a)
        for s in range(SUBLANES):
            for sl in range(NS):
                a_s[bi * NS + sl, s * P:s * P + J, :] = a[s * J:(s + 1) * J, sl * LANES:(sl + 1) * LANES]
                b_s[bi * NS + sl, s * P:s * P + J, :] = b_in[s * J:(s + 1) * J, sl * LANES:(sl + 1) * LANES]
        acc = merge(0, o_gm, None)
        yield "scan"
        fin = scan_out["fin"]

        h_cols = []
        for sl in range(NS):
            k = bi * NS + sl
            hl_end, ac_end = fin[2 * k], fin[2 * k + 1]
            hin = hcar[bi, cur, 0:1, sl * LANES:(sl + 1) * LANES]
            segs = []
            for s in range(SUBLANES):
                segs.append(hl_s[k, s * P:s * P + J, :] + ac_s[k, s * P:s * P + J, :] * hin)
                hin = hl_end[s:s + 1, :] + ac_end[s:s + 1, :] * hin
            hcar[bi, nxt, 0:1, sl * LANES:(sl + 1) * LANES] = hin
            h_cols.append(jnp.concatenate(segs, axis=0))
        h_lru = jnp.concatenate(h_cols, axis=1)
        o_lru = (_gelu(gb) * h_lru).astype(BF16)
        yield
        acc = merge(1, o_lru, acc)
        yield

        zq = proj(OFF_QKV, OFF_QKV + 768)
        vv = zq[:, 640:768]
        cos_t, sa_t, sb_t = cos_ref[...], sa_ref[...], sb_ref[...]
        roped = []
        for p in range(5):
            xt = zq[:, p * LANES:(p + 1) * LANES]
            ssq = _dot((xt * xt).astype(BF16), ones_ref[...])
            xt = xt * lax.rsqrt(ssq * (1.0 / HEAD_DIM) + EPS) * qkg_ref[:, p * LANES:(p + 1) * LANES]
            roped.append(xt * cos_t + pltpu.roll(xt, LANES - ROT_DIM // 2, 1) * sa_t
                         + pltpu.roll(xt, ROT_DIM // 2, 1) * sb_t)
        yield
        q_lo = [jnp.where(low_half, roped[p], 0.0).astype(BF16) for p in range(4)]
        q_hi = [jnp.where(low_half, 0.0, roped[p]).astype(BF16) for p in range(4)]
        q_same = q_lo[:half] + q_hi[half:]
        q_swap = q_hi[:half] + q_lo[half:]
        ones_t = jnp.ones((T, LANES), BF16)
        k2 = jnp.concatenate([roped[4].astype(BF16),
                              pltpu.roll(roped[4], HEAD_DIM, 1).astype(BF16)], axis=1)
        v3 = jnp.concatenate([vv.astype(BF16), ones_t,
                              pltpu.roll(vv, HEAD_DIM, 1).astype(BF16)], axis=1)
        kext = jnp.concatenate([kprev[bi, cur], k2], axis=0)
        vext = jnp.concatenate([vprev[bi, cur], v3], axis=0)
        kprev[bi, nxt] = k2[T - WINDOW:, :]
        vprev[bi, nxt] = v3[T - WINDOW:, :]
        probs = []
        for n in range(nw):
            rows = slice(n * WINDOW, (n + 1) * WINDOW)
            krows = slice(n * WINDOW, (n + 2) * WINDOW)
            bias_w = bias0 if n == 0 else bias
            probs.append((scores(jnp.concatenate([q[rows] for q in q_same], axis=0),
                                 kext[krows, :LANES], sink_ref[0], bias_w),
                          scores(jnp.concatenate([q[rows] for q in q_swap], axis=0),
                                 kext[krows, LANES:], sink_ref[1], bias_w)))
            yield
        o_win = []
        for n in range(nw):
            krows = slice(n * WINDOW, (n + 2) * WINDOW)
            (p_same, t_same), (p_swap, t_swap) = probs[n]
            o_same = values(p_same, t_same, vext[krows, :2 * LANES])
            o_swap = values(p_swap, t_swap,
                            jnp.concatenate([vext[krows, 2 * LANES:], vext[krows, LANES:2 * LANES]], axis=1))
            tiles = []
            for p in range(4):
                blk = slice(p * WINDOW, (p + 1) * WINDOW)
                lo, hi = (o_same[blk], o_swap[blk]) if p < half else (o_swap[blk], o_same[blk])
                tiles.append(jnp.where(lane_w, lo, hi))
            o_win.append(jnp.concatenate(tiles, axis=1))
            yield
        o_swa = jnp.concatenate(o_win, axis=0).astype(BF16)
        acc = merge(2, o_swa, acc)
        yield

        zx = proj(OFF_QX, OFF_QX + BRANCH_W)
        xg = xag_ref[...] * (XA_DIM ** -0.5)
        prs = []
        for hd in range(XA_HEADS):
            qn = _rms(zx[:, hd * XA_DIM:(hd + 1) * XA_DIM], xg, XA_DIM).astype(BF16)
            s = _dot(qn, kt_ref[bi, hd])
            prs.append(jnp.exp(s - jnp.max(s, axis=-1, keepdims=True)).astype(BF16))
            if hd % 2:
                yield
        o_heads = []
        for hd in range(XA_HEADS):
            oe = _dot(prs[hd], vx_ref[bi, hd])
            o_heads.append(oe[:, :XA_DIM] / oe[:, XA_DIM:])
            if hd % 2:
                yield
        o_xa = jnp.concatenate(o_heads, axis=1).astype(BF16)

        acc = merge(3, o_xa, acc)
        yield
        o_ref[bi] = x + _dot(acc.astype(BF16), w_out_ref[...])

    def scan_body(j, carry):
        new = []
        for k in range(NB * NS):
            hl, ac = carry[2 * k], carry[2 * k + 1]
            aj = a_s[k, pl.ds(j, SUBLANES, stride=P), :]
            bj = b_s[k, pl.ds(j, SUBLANES, stride=P), :]
            hl = aj * hl + bj
            ac = aj * ac
            hl_s[k, pl.ds(j, SUBLANES, stride=P), :] = hl
            ac_s[k, pl.ds(j, SUBLANES, stride=P), :] = ac
            new += [hl, ac]
        return tuple(new)

    gens = [stream(bi) for bi in range(NB)]
    done = [False] * NB
    at_scan = [False] * NB
    steps = [0] * NB

    def advance(i):
        try:
            if next(gens[i]) == "scan":
                at_scan[i] = True
            steps[i] += 1
        except StopIteration:
            done[i] = True

    def run_until(pred):
        while not pred():
            for i in range(NB):
                lead_ok = i == 0 or done[i - 1] or at_scan[i - 1] or steps[i - 1] - steps[i] >= MIX_LAG
                if not done[i] and not at_scan[i] and lead_ok:
                    advance(i)

    run_until(lambda: all(at_scan))
    init = tuple(jnp.zeros((SUBLANES, LANES), F32) if i % 2 == 0 else jnp.ones((SUBLANES, LANES), F32)
                 for i in range(2 * NB * NS))
    scan_out["fin"] = lax.fori_loop(0, J, scan_body, init, unroll=SCAN_UNROLL)
    at_scan[:] = [False] * NB
    steps[:] = [0] * NB
    run_until(lambda: all(done))


def _mixer(l, x, rope, kt, vx, w_in, w_br, w_out, stacked, shared, w_ff1, w_ff2):
    B, S, D = x.shape
    T = T_MIX
    nt = S // T
    J = T // SUBLANES
    P = J + SUBLANES
    NS = BRANCH_W // LANES
    vmem = pl.BlockSpec(memory_space=pltpu.VMEM)
    tok = pl.BlockSpec((B, T, D), lambda t: (0, t, 0))
    rp = pl.BlockSpec((T, LANES), lambda t: (t, 0))
    w1_in, w1_out = _row_block_specs(w_ff1.shape, l, nt, lambda t: t)
    w2_in, w2_out = _row_block_specs(w_ff2.shape, l, nt, lambda t: t)
    s = stacked
    operands = [
        (x, tok), (rope[0], rp), (rope[1], rp), (rope[2], rp),
        (kt, pl.BlockSpec((None, B, XA_HEADS, XA_DIM, N_MEM), lambda t: (l, 0, 0, 0, 0))),
        (vx, pl.BlockSpec((None, B, XA_HEADS, N_MEM, 2 * XA_DIM), lambda t: (l, 0, 0, 0, 0))),
        (w_in, vmem), (w_br, vmem), (w_out, vmem),
    ]
    for arr in (s["norm_mix"], s["gm_v_gain"], s["gm_ws"], s["gm_bias"],
                s["conv_w"], s["conv_b"], s["w_ri"], s["lru_br"], s["lru_bi"], s["lam"], s["qk_gain"]):
        operands.append((arr, _layer_spec(arr.shape, l)))
    operands.append((shared["ones_tile"], vmem))
    for arr in (s["sinks"], s["xa_q_gain"], s["b_gate"]):
        operands.append((arr, _layer_spec(arr.shape, l)))
    operands += [(w_ff1, w1_in), (w_ff2, w2_in)]
    return pl.pallas_call(
        functools.partial(_mixer_kernel, T=T, NB=B),
        grid=(nt,),
        in_specs=[spec for _, spec in operands],
        out_specs=[tok, w1_out, w2_out],
        out_shape=[jax.ShapeDtypeStruct((B, S, D), F32),
                   jax.ShapeDtypeStruct(w_ff1.shape[1:], BF16),
                   jax.ShapeDtypeStruct(w_ff2.shape[1:], BF16)],
        scratch_shapes=[
            pltpu.VMEM((B, 2, WINDOW, 2 * LANES), BF16),
            pltpu.VMEM((B, 2, WINDOW, 3 * LANES), BF16),
            pltpu.VMEM((B, 2, SUBLANES, BRANCH_W), F32),
            pltpu.VMEM((B, 2, SUBLANES, BRANCH_W), F32),
            pltpu.VMEM((B, T + SUBLANES, BRANCH_W), F32),
            pltpu.VMEM((B * NS, SUBLANES * P, LANES), F32),
            pltpu.VMEM((B * NS, SUBLANES * P, LANES), F32),
            pltpu.VMEM((B * NS, SUBLANES * P, LANES), F32),
            pltpu.VMEM((B * NS, SUBLANES * P, LANES), F32),
        ],
        compiler_params=pltpu.CompilerParams(
            dimension_semantics=("arbitrary",),
            vmem_limit_bytes=VMEM_LIMIT),
        name="mixer",
    )(*[arr for arr, _ in operands])


def _ffn_kernel(x_ref, g_ref, w1_ref, w2_ref, *rest):
    n_next = (len(rest) - 1) // 2
    o_ref = rest[n_next]
    _convert_rows(zip(rest[:n_next], rest[n_next + 1:]))
    half = x_ref.shape[0] // 2
    for h in range(2):
        rows = slice(h * half, (h + 1) * half)
        x = x_ref[rows, :]
        hb = _rms(x, g_ref[...], D_MODEL).astype(BF16)
        acc = x
        for c in range(D_FF // D_MODEL):
            hid = _dot(hb, w1_ref[:, c * D_MODEL:(c + 1) * D_MODEL])
            hid = jnp.square(jnp.maximum(hid, 0.0)).astype(BF16)
            acc = acc + _dot(hid, w2_ref[c * D_MODEL:(c + 1) * D_MODEL, :])
        o_ref[rows, :] = acc


def _ffn(l, x2, g, w1, w2, next_f32):
    N, D = x2.shape
    steps = N // T_FFN
    vmem = pl.BlockSpec(memory_space=pltpu.VMEM)
    tok = pl.BlockSpec((T_FFN, D), lambda i: (i, 0))
    conv = [_row_block_specs(a.shape, l + 1, steps, lambda i: i) for a in next_f32]
    outs = pl.pallas_call(
        _ffn_kernel,
        grid=(steps,),
        in_specs=[tok, _layer_spec(g.shape, l), vmem, vmem] + [c[0] for c in conv],
        out_specs=[tok] + [c[1] for c in conv],
        out_shape=[jax.ShapeDtypeStruct((N, D), F32)]
        + [jax.ShapeDtypeStruct(a.shape[1:], BF16) for a in next_f32],
        compiler_params=pltpu.CompilerParams(
            dimension_semantics=("arbitrary",), vmem_limit_bytes=VMEM_LIMIT),
        name="ffn",
    )(x2, g, w1, w2, *next_f32)
    return outs[0], outs[1:]


def _rope_tables(seq):
    pos = jnp.arange(seq, dtype=F32)
    half = ROT_DIM // 2
    inv = ROPE_THETA ** (-jnp.arange(0, ROT_DIM, 2, dtype=F32) / ROT_DIM)
    ang = pos[:, None] * inv[None, :]
    cos, sin = jnp.cos(ang), jnp.sin(ang)
    d = np.arange(LANES) % HEAD_DIM
    f = np.arange(half)[:, None]
    e_cos = ((d[None, :] < ROT_DIM) & (d[None, :] % half == f)).astype(np.float32)
    e_a = -((d[None, :] < half) & (d[None, :] == f)).astype(np.float32)
    e_b = ((d[None, :] >= half) & (d[None, :] < ROT_DIM) & (d[None, :] - half == f)).astype(np.float32)
    expand = lambda tbl, e: jnp.dot(tbl, jnp.asarray(e), precision=lax.Precision.HIGHEST)
    return (expand(cos, e_cos) + jnp.asarray((d >= ROT_DIM).astype(np.float32))[None, :],
            expand(sin, e_a), expand(sin, e_b))


@jax.jit
def kernel(x, mem, norm_mix, norm_mem, norm_mlp, w_in, b_gate, gm_v_gain, gm_ws, gm_bs, lru_conv_w, lru_conv_b, lru_wr, lru_br, lru_wi, lru_bi, lru_lambda, swa_q_gain, swa_k_gain, swa_sinks, w_mem_kv, xa_q_gain, xa_k_gain, w_branch, w_out, w_ff1, w_ff2):
    B, S, D = x.shape
    L = DEPTH
    row = lambda a: a[:, None, :]

    hb_n = LRU_BLOCKS // 2
    eye = jnp.eye(hb_n, dtype=F32)
    bd = lambda w: (eye[None, None, :, None, :, None]
                    * w.reshape(L, 2, hb_n, BRANCH_W // LRU_BLOCKS, 1, BRANCH_W // LRU_BLOCKS)
                    ).reshape(L, 2, BRANCH_W // 2, BRANCH_W // 2)
    same_heads = np.array([0, 2, 5, 7])
    swap_heads = np.array([1, 3, 4, 6])
    sinks = jnp.stack([jnp.repeat(swa_sinks[:, same_heads], WINDOW, axis=1),
                       jnp.repeat(swa_sinks[:, swap_heads], WINDOW, axis=1)], axis=1)[..., None]
    stacked = dict(
        norm_mix=row(norm_mix), gm_v_gain=row(gm_v_gain), gm_ws=gm_ws,
        gm_bias=jnp.broadcast_to(jnp.swapaxes(gm_bs, 1, 2)[:, :, :, None],
                                 (L, GM_CHUNK, GM_GROUPS, LANES)).reshape(L, GM_CHUNK, BRANCH_W),
        conv_w=lru_conv_w, conv_b=row(lru_conv_b),
        w_ri=jnp.concatenate([bd(lru_wr), bd(lru_wi)], axis=3).astype(BF16),
        lru_br=row(lru_br), lru_bi=row(lru_bi), lam=row(lru_lambda),
        qk_gain=row(jnp.concatenate([jnp.tile(swa_q_gain, (1, SWA_HEADS)) * (HEAD_DIM ** -0.5),
                                     jnp.tile(swa_k_gain, (1, SWA_KV))], axis=1)),
        sinks=sinks,
        xa_q_gain=row(xa_q_gain), b_gate=b_gate,
    )
    shared = dict(ones_tile=jnp.asarray(
        np.kron(np.eye(2, dtype=np.float32), np.ones((HEAD_DIM, HEAD_DIM), np.float32)), BF16))
    rope = _rope_tables(S)
    norm_mlp_r = row(norm_mlp)
    w_br2d = w_branch.reshape(L, N_BRANCH * BRANCH_W, D_MODEL)

    kt, vx = _mem_kv(mem, norm_mem, w_mem_kv.astype(BF16), xa_k_gain)

    mix_w = (w_in[0].astype(BF16), w_br2d[0].astype(BF16), w_out[0].astype(BF16))
    for l in range(L):
        w_in_l, w_br_l, w_out_l = mix_w
        x, w1_l, w2_l = _mixer(l, x, rope, kt, vx, w_in_l, w_br_l.reshape(N_BRANCH, BRANCH_W, D_MODEL),
                               w_out_l, stacked, shared, w_ff1, w_ff2)
        nxt = (w_in, w_br2d, w_out) if l + 1 < L else ()
        x2, mix_w = _ffn(l, x.reshape(B * S, D), norm_mlp_r, w1_l, w2_l, nxt)
        x = x2.reshape(B, S, D)
    return x
```

```python
import functools

import numpy as np
import jax
import jax.numpy as jnp
from jax import lax
from jax.experimental import pallas as pl
from jax.experimental.pallas import tpu as pltpu

D_MODEL = 1024
DEPTH = 4
N_MEM = 256
EPS = 1e-6
N_BRANCH = 4
BRANCH_W = 512
GM_GROUPS = 4
GM_CHUNK = 128
LRU_BLOCKS = 8
LRU_CONV = 4
LRU_C = 8.0
HEAD_DIM = 64
SWA_HEADS = 8
SWA_KV = 2
WINDOW = 128
ROPE_THETA = 500000.0
ROT_DIM = 16
XA_HEADS = 4
XA_DIM = 128
D_FF = 4 * D_MODEL

OFF_GM = 0
OFF_LRU = 1024
OFF_QKV = 2048
OFF_QX = 2816
OFF_GATE = 3328
D_IN = 7424

LANES = 128
SUBLANES = 8
T_MIX = 256
T_FFN = 1024
MIX_LAG = 2
SCAN_UNROLL = 4
VMEM_LIMIT = 56 * 1024 * 1024

F32 = jnp.float32
BF16 = jnp.bfloat16


def _dot(a, b):
    return jnp.dot(a, b, preferred_element_type=F32)


def _rms(x, gain_row, n):
    ms = jnp.sum(x * x, axis=-1, keepdims=True) * (1.0 / n)
    return x * lax.rsqrt(ms + EPS) * gain_row


GELU_C0 = 0.7978845608028654
GELU_C1 = GELU_C0 * 0.044715


def _gelu(x):
    hx = 0.5 * x
    return hx + hx * jnp.tanh(x * (GELU_C0 + GELU_C1 * (x * x)))


def _sigmoid(x):
    return 0.5 * jnp.tanh(0.5 * x) + 0.5


def _convert_rows(pairs):
    for src, dst in pairs:
        dst[...] = src[...].astype(BF16)


BF16_ROWS = 16


def _row_block_specs(shape, l, steps, flat_step):
    _, rows, cols = shape
    rb = rows // steps
    assert rb * steps == rows and rb % BF16_ROWS == 0, (shape, steps)
    return (pl.BlockSpec((None, rb, cols), lambda *idx: (l, flat_step(*idx), 0)),
            pl.BlockSpec((rb, cols), lambda *idx: (flat_step(*idx), 0)))


def _layer_spec(shape, l):
    nd = len(shape)
    return pl.BlockSpec((None,) + tuple(shape[1:]), lambda *_: (l,) + (0,) * (nd - 1),
                        pipeline_mode=pl.Buffered(1))


def _mem_kernel(mem_ref, nmem_ref, wkv_ref, kg_ref, kt_ref, vx_ref):
    mn = _rms(mem_ref[0], nmem_ref[0], D_MODEL).astype(BF16)
    kv = _dot(mn, wkv_ref[0])
    ones = jnp.ones((N_MEM, XA_DIM), BF16)
    for hd in range(XA_HEADS):
        kh = _rms(kv[:, hd * XA_DIM:(hd + 1) * XA_DIM], kg_ref[0], XA_DIM)
        kt_ref[0, 0, hd] = kh.T.astype(BF16)
        vh = kv[:, BRANCH_W + hd * XA_DIM:BRANCH_W + (hd + 1) * XA_DIM].astype(BF16)
        vx_ref[0, 0, hd] = jnp.concatenate([vh, ones], axis=1)


def _mem_kv(mem, norm_mem, w_mem_kv_bf, xa_k_gain):
    L, B = DEPTH, mem.shape[0]
    return pl.pallas_call(
        _mem_kernel,
        grid=(L, B),
        in_specs=[
            pl.BlockSpec((1, N_MEM, D_MODEL), lambda l, b: (b, 0, 0)),
            pl.BlockSpec((1, 1, D_MODEL), lambda l, b: (l, 0, 0)),
            pl.BlockSpec((1, D_MODEL, 2 * BRANCH_W), lambda l, b: (l, 0, 0)),
            pl.BlockSpec((1, 1, XA_DIM), lambda l, b: (l, 0, 0)),
        ],
        out_specs=[
            pl.BlockSpec((1, 1, XA_HEADS, XA_DIM, N_MEM), lambda l, b: (l, b, 0, 0, 0)),
            pl.BlockSpec((1, 1, XA_HEADS, N_MEM, 2 * XA_DIM), lambda l, b: (l, b, 0, 0, 0)),
        ],
        out_shape=[
            jax.ShapeDtypeStruct((L, B, XA_HEADS, XA_DIM, N_MEM), BF16),
            jax.ShapeDtypeStruct((L, B, XA_HEADS, N_MEM, 2 * XA_DIM), BF16),
        ],
        compiler_params=pltpu.CompilerParams(
            dimension_semantics=("arbitrary", "arbitrary")),
        name="mem_kv",
    )(mem, norm_mem.reshape(L, 1, D_MODEL), w_mem_kv_bf, xa_k_gain.reshape(L, 1, XA_DIM))


def _mixer_kernel(x_ref, cos_ref, sa_ref, sb_ref, kt_ref, vx_ref,
                  w_in_ref, w_br_ref, w_out_ref,
                  nmix_ref, gmv_ref, gmws_ref, gmb_ref,
                  cw_ref, cb_ref, wri_ref, br_ref, bi_ref, lam_ref,
                  qkg_ref, ones_ref, sink_ref, xag_ref, bg_ref, w1f_ref, w2f_ref,
                  o_ref, w1b_ref, w2b_ref,
                  kprev, vprev, xtail, hcar, xbuf, a_s, b_s, hl_s, ac_s, *, T, NB):
    t = pl.program_id(0)
    _convert_rows(((w1f_ref, w1b_ref), (w2f_ref, w2b_ref)))
    nw = T // WINDOW
    J = T // SUBLANES
    P = J + SUBLANES
    NS = BRANCH_W // LANES

    cur = lax.rem(t, 2)
    nxt = 1 - cur

    @pl.when(t == 0)
    def _():
        for bi in range(NB):
            kprev[bi, 0] = jnp.zeros((WINDOW, 2 * LANES), BF16)
            vprev[bi, 0] = jnp.zeros((WINDOW, 3 * LANES), BF16)
            xtail[bi, 0] = jnp.zeros((SUBLANES, BRANCH_W), F32)
            hcar[bi, 0] = jnp.zeros((SUBLANES, BRANCH_W), F32)

    r_i = lax.broadcasted_iota(jnp.int32, (GM_CHUNK, GM_CHUNK), 0)
    c_i = lax.broadcasted_iota(jnp.int32, (GM_CHUNK, GM_CHUNK), 1)
    causal = r_i >= c_i
    slot_rows = (SWA_HEADS // 2) * WINDOW
    own_key = (lax.broadcasted_iota(jnp.int32, (slot_rows, WINDOW), 0) & (WINDOW - 1)
               ) >= lax.broadcasted_iota(jnp.int32, (slot_rows, WINDOW), 1)
    no_prev = jnp.where(t > 0, 0.0, -jnp.inf)
    lane_w = lax.broadcasted_iota(jnp.int32, (WINDOW, LANES), 1) < HEAD_DIM
    low_half = lax.broadcasted_iota(jnp.int32, (T, LANES), 1) < HEAD_DIM
    half = SWA_HEADS // SWA_KV // 2
    scan_out = {}

    def scores(qs, kw, sink, first):
        s = lax.dot_general(qs, kw, (((1,), (1,)), ((), ())), preferred_element_type=F32)
        s_prev = s[:, :WINDOW] + no_prev if first else s[:, :WINDOW]
        sc = jnp.where(own_key, s[:, WINDOW:], s_prev)
        m = jnp.maximum(jnp.max(sc, axis=-1, keepdims=True), sink)
        p = jnp.exp(sc - m)
        pr = jnp.concatenate([jnp.where(own_key, 0.0, p), jnp.where(own_key, p, 0.0)], axis=1)
        return pr.astype(BF16), jnp.exp(sink - m)

    def values(pr, tail, vw):
        oe = _dot(pr, vw)
        return oe[:, :LANES] / (oe[:, LANES:] + tail)

    def stream(bi):
        x = x_ref[bi]
        hb = _rms(x, nmix_ref[...], D_MODEL).astype(BF16)

        def proj(lo, hi):
            return _dot(hb, w_in_ref[:, lo:hi])

        def merge(b, ob, acc):
            gate = _sigmoid(
                proj(OFF_GATE + b * D_MODEL, OFF_GATE + (b + 1) * D_MODEL) + bg_ref[b:b + 1, :])
            contrib = gate * _dot(ob, w_br_ref[b])
            return contrib if acc is None else acc + contrib
        yield

        zg_raw = proj(OFF_GM, OFF_GM + 2 * BRANCH_W)
        zl = proj(OFF_LRU, OFF_LRU + 2 * BRANCH_W)
        yield
        zg = _gelu(zg_raw)
        u = zg[:, :BRANCH_W]
        vn = _rms(zg[:, BRANCH_W:], gmv_ref[...], BRANCH_W).astype(BF16)
        s_cols = []
        for g in range(GM_GROUPS):
            wg = jnp.where(causal, gmws_ref[g], 0.0).astype(BF16)
            vcat = jnp.concatenate(
                [vn[c * GM_CHUNK:(c + 1) * GM_CHUNK, g * LANES:(g + 1) * LANES] for c in range(nw)], axis=1)
            sg = _dot(wg, vcat)
            s_cols.append(jnp.concatenate(
                [sg[:, c * LANES:(c + 1) * LANES] for c in range(nw)], axis=0))
        s_full = jnp.concatenate(s_cols, axis=1) + jnp.concatenate([gmb_ref[...]] * nw, axis=0)
        o_gm = (u * s_full).astype(BF16)
        yield

        xb = zl[:, :BRANCH_W]
        gb = zl[:, BRANCH_W:]
        xbuf[bi, 0:SUBLANES, :] = xtail[bi, cur]
        xbuf[bi, SUBLANES:SUBLANES + T, :] = xb
        xtail[bi, nxt] = xb[T - SUBLANES:, :]
        yield
        xc = cb_ref[...] + xb * cw_ref[3:4, :]
        for k in range(LRU_CONV - 1):
            sh = LRU_CONV - 1 - k
            xc = xc + xbuf[bi, SUBLANES - sh:SUBLANES - sh + T, :] * cw_ref[k:k + 1, :]
        xcb = xc.astype(BF16)
        hw = BRANCH_W // 2
        ri = [_dot(xcb[:, h * hw:(h + 1) * hw], wri_ref[h]) for h in range(2)]
        yield
        r_gate = _sigmoid(jnp.concatenate([ri[0][:, :hw], ri[1][:, :hw]], axis=1) + br_ref[...])
        i_gate = _sigmoid(jnp.concatenate([ri[0][:, hw:], ri[1][:, hw:]], axis=1) + bi_ref[...])
        nl = -lam_ref[...]
        softplus = jnp.maximum(nl, 0.0) + jnp.log1p(jnp.exp(-jnp.abs(nl)))
        log_a = (-LRU_C) * r_gate * softplus
        a = jnp.exp(log_a)
        b_in = xc * i_gate * jnp.sqrt(1.0 - a * a)
        for s in range(SUBLANES):
            for sl in range(NS):
                a_s[bi * NS + sl, s * P:s * P + J, :] = a[s * J:(s + 1) * J, sl * LANES:(sl + 1) * LANES]
                b_s[bi * NS + sl, s * P:s * P + J, :] = b_in[s * J:(s + 1) * J, sl * LANES:(sl + 1) * LANES]
        acc = merge(0, o_gm, None)
        yield "scan"
        fin = scan_out["fin"]

        h_cols = []
        for sl in range(NS):
            k = bi * NS + sl
            hl_end, ac_end = fin[2 * k], fin[2 * k + 1]
            hin = hcar[bi, cur, 0:1, sl * LANES:(sl + 1) * LANES]
            segs = []
            for s in range(SUBLANES):
                segs.append(hl_s[k, s * P:s * P + J, :] + ac_s[k, s * P:s * P + J, :] * hin)
                hin = hl_end[s:s + 1, :] + ac_end[s:s + 1, :] * hin
            hcar[bi, nxt, 0:1, sl * LANES:(sl + 1) * LANES] = hin
            h_cols.append(jnp.concatenate(segs, axis=0))
        h_lru = jnp.concatenate(h_cols, axis=1)
        o_lru = (_gelu(gb) * h_lru).astype(BF16)
        yield
        acc = merge(1, o_lru, acc)
        yield

        zq = proj(OFF_QKV, OFF_QKV + 768)
        vv = zq[:, 640:768]
        cos_t, sa_t, sb_t = cos_ref[...], sa_ref[...], sb_ref[...]
        roped = []
        for p in range(5):
            xt = zq[:, p * LANES:(p + 1) * LANES]
            ssq = _dot((xt * xt).astype(BF16), ones_ref[...])
            xt = xt * lax.rsqrt(ssq * (1.0 / HEAD_DIM) + EPS) * qkg_ref[:, p * LANES:(p + 1) * LANES]
            roped.append(xt * cos_t + pltpu.roll(xt, LANES - ROT_DIM // 2, 1) * sa_t
                         + pltpu.roll(xt, ROT_DIM // 2, 1) * sb_t)
        yield
        q_lo = [jnp.where(low_half, roped[p], 0.0).astype(BF16) for p in range(4)]
        q_hi = [jnp.where(low_half, 0.0, roped[p]).astype(BF16) for p in range(4)]
        q_same = q_lo[:half] + q_hi[half:]
        q_swap = q_hi[:half] + q_lo[half:]
        ones_t = jnp.ones((T, LANES), BF16)
        k2 = jnp.concatenate([roped[4].astype(BF16),
                              pltpu.roll(roped[4], HEAD_DIM, 1).astype(BF16)], axis=1)
        v3 = jnp.concatenate([vv.astype(BF16), ones_t,
                              pltpu.roll(vv, HEAD_DIM, 1).astype(BF16)], axis=1)
        kext = jnp.concatenate([kprev[bi, cur], k2], axis=0)
        vext = jnp.concatenate([vprev[bi, cur], v3], axis=0)
        kprev[bi, nxt] = k2[T - WINDOW:, :]
        vprev[bi, nxt] = v3[T - WINDOW:, :]
        probs = []
        for n in range(nw):
            rows = slice(n * WINDOW, (n + 1) * WINDOW)
            krows = slice(n * WINDOW, (n + 2) * WINDOW)
            probs.append((scores(jnp.concatenate([q[rows] for q in q_same], axis=0),
                                 kext[krows, :LANES], sink_ref[0], n == 0),
                          scores(jnp.concatenate([q[rows] for q in q_swap], axis=0),
                                 kext[krows, LANES:], sink_ref[1], n == 0)))
            yield
        o_win = []
        for n in range(nw):
            krows = slice(n * WINDOW, (n + 2) * WINDOW)
            (p_same, t_same), (p_swap, t_swap) = probs[n]
            o_same = values(p_same, t_same, vext[krows, :2 * LANES])
            o_swap = values(p_swap, t_swap,
                            jnp.concatenate([vext[krows, 2 * LANES:], vext[krows, LANES:2 * LANES]], axis=1))
            tiles = []
            for p in range(4):
                blk = slice(p * WINDOW, (p + 1) * WINDOW)
                lo, hi = (o_same[blk], o_swap[blk]) if p < half else (o_swap[blk], o_same[blk])
                tiles.append(jnp.where(lane_w, lo, hi))
            o_win.append(jnp.concatenate(tiles, axis=1))
            yield
        o_swa = jnp.concatenate(o_win, axis=0).astype(BF16)
        acc = merge(2, o_swa, acc)
        yield

        zx = proj(OFF_QX, OFF_QX + BRANCH_W)
        xg = xag_ref[...] * (XA_DIM ** -0.5)
        prs = []
        for hd in range(XA_HEADS):
            qn = _rms(zx[:, hd * XA_DIM:(hd + 1) * XA_DIM], xg, XA_DIM).astype(BF16)
            s = _dot(qn, kt_ref[bi, hd])
            prs.append(jnp.exp(s - jnp.max(s, axis=-1, keepdims=True)).astype(BF16))
            if hd % 2:
                yield
        o_heads = []
        for hd in range(XA_HEADS):
            oe = _dot(prs[hd], vx_ref[bi, hd])
            o_heads.append(oe[:, :XA_DIM] / oe[:, XA_DIM:])
            if hd % 2:
                yield
        o_xa = jnp.concatenate(o_heads, axis=1).astype(BF16)

        acc = merge(3, o_xa, acc)
        yield
        o_ref[bi] = x + _dot(acc.astype(BF16), w_out_ref[...])

    def scan_body(j, carry):
        new = []
        for k in range(NB * NS):
            hl, ac = carry[2 * k], carry[2 * k + 1]
            aj = a_s[k, pl.ds(j, SUBLANES, stride=P), :]
            bj = b_s[k, pl.ds(j, SUBLANES, stride=P), :]
            hl = aj * hl + bj
            ac = aj * ac
            hl_s[k, pl.ds(j, SUBLANES, stride=P), :] = hl
            ac_s[k, pl.ds(j, SUBLANES, stride=P), :] = ac
            new += [hl, ac]
        return tuple(new)

    gens = [stream(bi) for bi in range(NB)]
    done = [False] * NB
    at_scan = [False] * NB
    steps = [0] * NB

    def advance(i):
        try:
            if next(gens[i]) == "scan":
                at_scan[i] = True
            steps[i] += 1
        except StopIteration:
            done[i] = True

    def run_until(pred):
        while not pred():
            for i in range(NB):
                lead_ok = i == 0 or done[i - 1] or at_scan[i - 1] or steps[i - 1] - steps[i] >= MIX_LAG
                if not done[i] and not at_scan[i] and lead_ok:
                    advance(i)

    run_until(lambda: all(at_scan))
    init = tuple(jnp.zeros((SUBLANES, LANES), F32) if i % 2 == 0 else jnp.ones((SUBLANES, LANES), F32)
                 for i in range(2 * NB * NS))
    scan_out["fin"] = lax.fori_loop(0, J, scan_body, init, unroll=SCAN_UNROLL)
    at_scan[:] = [False] * NB
    steps[:] = [0] * NB
    run_until(lambda: all(done))


def _mixer(l, x, rope, kt, vx, w_in, w_br, w_out, stacked, shared, w_ff1, w_ff2):
    B, S, D = x.shape
    T = T_MIX
    nt = S // T
    J = T // SUBLANES
    P = J + SUBLANES
    NS = BRANCH_W // LANES
    vmem = pl.BlockSpec(memory_space=pltpu.VMEM)
    tok = pl.BlockSpec((B, T, D), lambda t: (0, t, 0))
    rp = pl.BlockSpec((T, LANES), lambda t: (t, 0))
    w1_in, w1_out = _row_block_specs(w_ff1.shape, l, nt, lambda t: t)
    w2_in, w2_out = _row_block_specs(w_ff2.shape, l, nt, lambda t: t)
    s = stacked
    operands = [
        (x, tok), (rope[0], rp), (rope[1], rp), (rope[2], rp),
        (kt, pl.BlockSpec((None, B, XA_HEADS, XA_DIM, N_MEM), lambda t: (l, 0, 0, 0, 0))),
        (vx, pl.BlockSpec((None, B, XA_HEADS, N_MEM, 2 * XA_DIM), lambda t: (l, 0, 0, 0, 0))),
        (w_in, vmem), (w_br, vmem), (w_out, vmem),
    ]
    for arr in (s["norm_mix"], s["gm_v_gain"], s["gm_ws"], s["gm_bias"],
                s["conv_w"], s["conv_b"], s["w_ri"], s["lru_br"], s["lru_bi"], s["lam"], s["qk_gain"]):
        operands.append((arr, _layer_spec(arr.shape, l)))
    operands.append((shared["ones_tile"], vmem))
    for arr in (s["sinks"], s["xa_q_gain"], s["b_gate"]):
        operands.append((arr, _layer_spec(arr.shape, l)))
    operands += [(w_ff1, w1_in), (w_ff2, w2_in)]
    return pl.pallas_call(
        functools.partial(_mixer_kernel, T=T, NB=B),
        grid=(nt,),
        in_specs=[spec for _, spec in operands],
        out_specs=[tok, w1_out, w2_out],
        out_shape=[jax.ShapeDtypeStruct((B, S, D), F32),
                   jax.ShapeDtypeStruct(w_ff1.shape[1:], BF16),
                   jax.ShapeDtypeStruct(w_ff2.shape[1:], BF16)],
        scratch_shapes=[
            pltpu.VMEM((B, 2, WINDOW, 2 * LANES), BF16),
            pltpu.VMEM((B, 2, WINDOW, 3 * LANES), BF16),
            pltpu.VMEM((B, 2, SUBLANES, BRANCH_W), F32),
            pltpu.VMEM((B, 2, SUBLANES, BRANCH_W), F32),
            pltpu.VMEM((B, T + SUBLANES, BRANCH_W), F32),
            pltpu.VMEM((B * NS, SUBLANES * P, LANES), F32),
            pltpu.VMEM((B * NS, SUBLANES * P, LANES), F32),
            pltpu.VMEM((B * NS, SUBLANES * P, LANES), F32),
            pltpu.VMEM((B * NS, SUBLANES * P, LANES), F32),
        ],
        compiler_params=pltpu.CompilerParams(
            dimension_semantics=("arbitrary",),
            vmem_limit_bytes=VMEM_LIMIT),
        name="mixer",
    )(*[arr for arr, _ in operands])


def _ffn_kernel(x_ref, g_ref, w1_ref, w2_ref, *rest):
    n_next = (len(rest) - 1) // 2
    o_ref = rest[n_next]
    _convert_rows(zip(rest[:n_next], rest[n_next + 1:]))
    half = x_ref.shape[0] // 2
    for h in range(2):
        rows = slice(h * half, (h + 1) * half)
        x = x_ref[rows, :]
        hb = _rms(x, g_ref[...], D_MODEL).astype(BF16)
        acc = x
        for c in range(D_FF // D_MODEL):
            hid = _dot(hb, w1_ref[:, c * D_MODEL:(c + 1) * D_MODEL])
            hid = jnp.square(jnp.maximum(hid, 0.0)).astype(BF16)
            acc = acc + _dot(hid, w2_ref[c * D_MODEL:(c + 1) * D_MODEL, :])
        o_ref[rows, :] = acc


def _ffn(l, x2, g, w1, w2, next_f32):
    N, D = x2.shape
    steps = N // T_FFN
    vmem = pl.BlockSpec(memory_space=pltpu.VMEM)
    tok = pl.BlockSpec((T_FFN, D), lambda i: (i, 0))
    conv = [_row_block_specs(a.shape, l + 1, steps, lambda i: i) for a in next_f32]
    outs = pl.pallas_call(
        _ffn_kernel,
        grid=(steps,),
        in_specs=[tok, _layer_spec(g.shape, l), vmem, vmem] + [c[0] for c in conv],
        out_specs=[tok] + [c[1] for c in conv],
        out_shape=[jax.ShapeDtypeStruct((N, D), F32)]
        + [jax.ShapeDtypeStruct(a.shape[1:], BF16) for a in next_f32],
        compiler_params=pltpu.CompilerParams(
            dimension_semantics=("arbitrary",), vmem_limit_bytes=VMEM_LIMIT),
        name="ffn",
    )(x2, g, w1, w2, *next_f32)
    return outs[0], outs[1:]


def _rope_tables(seq):
    pos = jnp.arange(seq, dtype=F32)
    half = ROT_DIM // 2
    inv = ROPE_THETA ** (-jnp.arange(0, ROT_DIM, 2, dtype=F32) / ROT_DIM)
    ang = pos[:, None] * inv[None, :]
    cos, sin = jnp.cos(ang), jnp.sin(ang)
    d = np.arange(LANES) % HEAD_DIM
    f = np.arange(half)[:, None]
    e_cos = ((d[None, :] < ROT_DIM) & (d[None, :] % half == f)).astype(np.float32)
    e_a = -((d[None, :] < half) & (d[None, :] == f)).astype(np.float32)
    e_b = ((d[None, :] >= half) & (d[None, :] < ROT_DIM) & (d[None, :] - half == f)).astype(np.float32)
    expand = lambda tbl, e: jnp.dot(tbl, jnp.asarray(e), precision=lax.Precision.HIGHEST)
    return (expand(cos, e_cos) + jnp.asarray((d >= ROT_DIM).astype(np.float32))[None, :],
            expand(sin, e_a), expand(sin, e_b))


@jax.jit
def kernel(x, mem, norm_mix, norm_mem, norm_mlp, w_in, b_gate, gm_v_gain, gm_ws, gm_bs, lru_conv_w, lru_conv_b, lru_wr, lru_br, lru_wi, lru_bi, lru_lambda, swa_q_gain, swa_k_gain, swa_sinks, w_mem_kv, xa_q_gain, xa_k_gain, w_branch, w_out, w_ff1, w_ff2):
    B, S, D = x.shape
    L = DEPTH
    row = lambda a: a[:, None, :]

    hb_n = LRU_BLOCKS // 2
    eye = jnp.eye(hb_n, dtype=F32)
    bd = lambda w: (eye[None, None, :, None, :, None]
                    * w.reshape(L, 2, hb_n, BRANCH_W // LRU_BLOCKS, 1, BRANCH_W // LRU_BLOCKS)
                    ).reshape(L, 2, BRANCH_W // 2, BRANCH_W // 2)
    same_heads = np.array([0, 2, 5, 7])
    swap_heads = np.array([1, 3, 4, 6])
    sinks = jnp.stack([jnp.repeat(swa_sinks[:, same_heads], WINDOW, axis=1),
                       jnp.repeat(swa_sinks[:, swap_heads], WINDOW, axis=1)], axis=1)[..., None]
    stacked = dict(
        norm_mix=row(norm_mix), gm_v_gain=row(gm_v_gain), gm_ws=gm_ws,
        gm_bias=jnp.broadcast_to(jnp.swapaxes(gm_bs, 1, 2)[:, :, :, None],
                                 (L, GM_CHUNK, GM_GROUPS, LANES)).reshape(L, GM_CHUNK, BRANCH_W),
        conv_w=lru_conv_w, conv_b=row(lru_conv_b),
        w_ri=jnp.concatenate([bd(lru_wr), bd(lru_wi)], axis=3).astype(BF16),
        lru_br=row(lru_br), lru_bi=row(lru_bi), lam=row(lru_lambda),
        qk_gain=row(jnp.concatenate([jnp.tile(swa_q_gain, (1, SWA_HEADS)) * (HEAD_DIM ** -0.5),
                                     jnp.tile(swa_k_gain, (1, SWA_KV))], axis=1)),
        sinks=sinks,
        xa_q_gain=row(xa_q_gain), b_gate=b_gate,
    )
    shared = dict(ones_tile=jnp.asarray(
        np.kron(np.eye(2, dtype=np.float32), np.ones((HEAD_DIM, HEAD_DIM), np.float32)), BF16))
    rope = _rope_tables(S)
    norm_mlp_r = row(norm_mlp)
    w_br2d = w_branch.reshape(L, N_BRANCH * BRANCH_W, D_MODEL)

    kt, vx = _mem_kv(mem, norm_mem, w_mem_kv.astype(BF16), xa_k_gain)

    mix_w = (w_in[0].astype(BF16), w_br2d[0].astype(BF16), w_out[0].astype(BF16))
    for l in range(L):
        w_in_l, w_br_l, w_out_l = mix_w
        x, w1_l, w2_l = _mixer(l, x, rope, kt, vx, w_in_l, w_br_l.reshape(N_BRANCH, BRANCH_W, D_MODEL),
                               w_out_l, stacked, shared, w_ff1, w_ff2)
        nxt = (w_in, w_br2d, w_out) if l + 1 < L else ()
        x2, mix_w = _ffn(l, x.reshape(B * S, D), norm_mlp_r, w1_l, w2_l, nxt)
        x = x2.reshape(B, S, D)
    return x
```
